```python
import math
import jax, jax.numpy as jnp
from jax import lax
import numpy as np

D_MODEL = 1024
BATCH = 4
SEQ = 4096
DEPTH = 4
DEC_BATCH = 2
DEC_SEQ = 16384
PAST_LEN = 128

N_EVEN = (DEPTH + 1) // 2
N_ODD = DEPTH // 2

HY_W = D_MODEL // 2
HY_ORDER = 2
HY_SHORT = 3
HY_BANDS = 16
HY_POS_DIM = 2 * HY_BANDS + 1
HY_FILT_HID = 64
HY_DECAY_TARGET = 1e-2
HY_FAST_DECAY = 0.3
HY_SLOW_DECAY = 1.5

MLA_HEADS = 8
MLA_NOPE = 64
MLA_ROPE = 32
MLA_V = 64
MLA_Q_LORA = 256
MLA_KV_LORA = 128
ROPE_THETA = 10000.0
Q_BLOCK = 128

IN_WIDTH = (HY_ORDER + 1) * HY_W + MLA_Q_LORA + MLA_KV_LORA + MLA_ROPE
MIX_OUT_W = HY_W + MLA_HEADS * MLA_V

RW_HEAD = 64
RW_HEADS = D_MODEL // RW_HEAD
RW_DECAY_LORA = 64
RW_AAA_LORA = 64
RW_GATE_LORA = 160
RW_GN_EPS = 64e-5

MOE_GROUPS = 4
MOE_PER_GROUP = 8
MOE_EXPERTS = MOE_GROUPS * MOE_PER_GROUP
MOE_TOPK = 2
MOE_FF = 512
MOE_BLOCK = 128

LN_EPS = 1e-5
RMS_EPS = 1e-6
DN_ALPHA = (2 * DEPTH) ** 0.25
DN_BETA = (8 * DEPTH) ** -0.25

kernel_name = "hyena_mla_rwkv7_hmoe_deepnorm_encoder"

F32 = jnp.float32


def _layer_norm(x, g, b):
    xf = x.astype(F32)
    mu = xf.mean(-1, keepdims=True)
    var = jnp.square(xf - mu).mean(-1, keepdims=True)
    return ((xf - mu) * lax.rsqrt(var + LN_EPS) * g.astype(F32) + b.astype(F32)).astype(x.dtype)


def _rms_norm(x, g):
    xf = x.astype(F32)
    return xf * lax.rsqrt(jnp.mean(xf * xf, -1, keepdims=True) + RMS_EPS) * g.astype(F32)


def _rope_tables(L):
    inv = ROPE_THETA ** (-jnp.arange(0, MLA_ROPE, 2, dtype=F32) / MLA_ROPE)
    ang = jnp.arange(L, dtype=F32)[:, None] * inv[None, :]
    return jnp.cos(ang), jnp.sin(ang)


def _apply_rope(x, cos, sin):
    x1, x2 = jnp.split(x.astype(F32), 2, axis=-1)
    return jnp.concatenate([x1 * cos - x2 * sin, x1 * sin + x2 * cos], axis=-1)


def _hyena_filters(L, w1, b1, w2, b2, w3, freq, decay):
    t = jnp.arange(L, dtype=F32) / L
    bands = jnp.linspace(1e-4, HY_BANDS - 1, HY_BANDS, dtype=F32)
    ph = 2.0 * math.pi * t[:, None] * bands[None, :]
    feat = jnp.concatenate([t[:, None], jnp.sin(ph), jnp.cos(ph)], axis=-1)
    fr = freq.astype(F32)
    h = jnp.sin(fr * (feat @ w1.astype(F32) + b1.astype(F32)))
    h = jnp.sin(fr * (h @ w2.astype(F32) + b2.astype(F32)))
    h = (h @ w3.astype(F32)).reshape(L, 2, HY_ORDER, HY_W)
    h = h * jnp.exp(-t[:, None, None, None] * jnp.abs(decay.astype(F32))[None])
    k = jnp.concatenate([h[:, 0], jnp.zeros_like(h[:1, 1]), h[:0:-1, 1]], axis=0)
    k = k / (jnp.sum(jnp.abs(k), axis=0, keepdims=True) + 1e-6)
    return jnp.fft.rfft(k, axis=0)


def _hyena(u, conv_w, conv_b, skip, filt_fft):
    L = u.shape[1]
    up = jnp.pad(u.astype(F32), ((0, 0), (1, 1), (0, 0)))
    cw = conv_w.astype(F32)
    u = up[:, :-2] * cw[0] + up[:, 1:-1] * cw[1] + up[:, 2:] * cw[2] + conv_b.astype(F32)
    parts = jnp.split(u, HY_ORDER + 1, axis=-1)
    z = parts[0]
    sk = skip.astype(F32)
    for n in range(HY_ORDER):
        zf = jnp.fft.rfft(z, n=2 * L, axis=1)
        conv = jnp.fft.irfft(zf * filt_fft[None, :, n], n=2 * L, axis=1)[:, :L]
        z = parts[n + 1] * (conv + sk[n] * z)
    return z


def _mla(c_q, c_kv, k_pe, q_norm, w_uq, kv_norm, w_ukv):
    B, L, _ = c_q.shape
    cos, sin = _rope_tables(L)
    q = (_rms_norm(c_q, q_norm) @ w_uq.astype(F32)).reshape(B, L, MLA_HEADS, MLA_NOPE + MLA_ROPE)
    q_nope = q[..., :MLA_NOPE]
    q_pe = _apply_rope(q[..., MLA_NOPE:], cos[:, None, :], sin[:, None, :])
    kv = (_rms_norm(c_kv, kv_norm) @ w_ukv.astype(F32)).reshape(B, L, MLA_HEADS, MLA_NOPE + MLA_V)
    k_nope, v = kv[..., :MLA_NOPE], kv[..., MLA_NOPE:]
    k_pe = _apply_rope(k_pe, cos, sin)
    scale = (MLA_NOPE + MLA_ROPE) ** -0.5
    nb = L // Q_BLOCK
    qn = q_nope.reshape(B, nb, Q_BLOCK, MLA_HEADS, MLA_NOPE).transpose(1, 0, 2, 3, 4)
    qp = q_pe.reshape(B, nb, Q_BLOCK, MLA_HEADS, MLA_ROPE).transpose(1, 0, 2, 3, 4)

    def block(args):
        qn_b, qp_b = args
        s = jnp.einsum('bqhd,bkhd->bhqk', qn_b, k_nope) + jnp.einsum('bqhr,bkr->bhqk', qp_b, k_pe)
        p = jax.nn.softmax(s * scale, axis=-1)
        return jnp.einsum('bhqk,bkhd->bqhd', p, v)

    o = lax.map(block, (qn, qp))
    return o.transpose(1, 0, 2, 3, 4).reshape(B, L, MLA_HEADS * MLA_V)


def _hyena_mla_layer(x, w_in, conv_w, conv_b, skip, filt_fft, q_norm, w_uq, kv_norm, w_ukv, w_out):
    h = x @ w_in
    s1 = (HY_ORDER + 1) * HY_W
    s2 = s1 + MLA_Q_LORA
    s3 = s2 + MLA_KV_LORA
    y_hy = _hyena(h[..., :s1], conv_w, conv_b, skip, filt_fft)
    y_mla = _mla(h[..., s1:s2], h[..., s2:s3], h[..., s3:], q_norm, w_uq, kv_norm, w_ukv)
    y = jnp.concatenate([y_hy, y_mla], axis=-1).astype(x.dtype)
    return y @ w_out


def _time_major(t):
    t = jnp.stack([t[0], jnp.flip(t[1], axis=1)])
    return jnp.moveaxis(t, 2, 0)


def _rwkv7(x, mu, w_r, w_k, w_v, w0, w1, w2, a0, a1, a2, g1, g2, k_k, k_a, r_k, gn_g, gn_b, w_o):
    B, L, D = x.shape
    H, N = RW_HEADS, RW_HEAD
    xf = x.astype(F32)
    zero = jnp.zeros_like(xf[:, :1])
    prev = jnp.concatenate([zero, xf[:, :-1]], axis=1)
    nxt = jnp.concatenate([xf[:, 1:], zero], axis=1)
    xx = jnp.stack([prev, nxt]) - xf[None]
    mu = mu.astype(F32)
    xr, xw, xk, xv, xa, xg = [xf[None] + xx * mu[:, i, None, None, :] for i in range(6)]
    r = jnp.einsum('zbld,de->zble', xr, w_r)
    k = jnp.einsum('zbld,de->zble', xk, w_k)
    v = jnp.einsum('zbld,de->zble', xv, w_v)
    lw = w0[:, None, None, :] + jnp.einsum('zblr,zrd->zbld', jnp.tanh(jnp.einsum('zbld,zdr->zblr', xw, w1)), w2)
    decay = jnp.exp(-jnp.exp(-jax.nn.softplus(-lw) - 0.5))
    a = jax.nn.sigmoid(a0[:, None, None, :] + jnp.einsum('zblr,zrd->zbld', jnp.einsum('zbld,zdr->zblr', xa, a1), a2))
    g = jnp.einsum('zblr,zrd->zbld', jax.nn.sigmoid(jnp.einsum('zbld,zdr->zblr', xg, g1)), g2)
    hs = (2, B, L, H, N)
    kk = (k * k_k).reshape(hs)
    kk = kk / jnp.maximum(jnp.sqrt(jnp.sum(kk * kk, -1, keepdims=True)), 1e-12)
    k = k * (1.0 + (a - 1.0) * k_a)
    r_h, k_h, v_h, a_h, w_h = r.reshape(hs), k.reshape(hs), v.reshape(hs), a.reshape(hs), decay.reshape(hs)

    def step(S, inp):
        r_t, w_t, k_t, v_t, kk_t, a_t = inp
        sa = jnp.einsum('zbhij,zbhj->zbhi', S, -kk_t)
        S = (S * w_t[..., None, :] + sa[..., :, None] * (kk_t * a_t)[..., None, :]
             + v_t[..., :, None] * k_t[..., None, :])
        return S, jnp.einsum('zbhij,zbhj->zbhi', S, r_t)

    S0 = jnp.zeros((2, B, H, N, N), F32)
    _, y = lax.scan(step, S0, tuple(_time_major(t) for t in (r_h, w_h, k_h, v_h, kk, a_h)))
    y = jnp.moveaxis(y, 0, 2)
    y = jnp.stack([y[0], jnp.flip(y[1], axis=1)])
    ym = y.mean(-1, keepdims=True)
    yv = jnp.square(y - ym).mean(-1, keepdims=True)
    yn = ((y - ym) * lax.rsqrt(yv + RW_GN_EPS)).reshape(2, B, L, D) * gn_g + gn_b
    bonus = (jnp.sum(r_h * k_h * r_k, -1, keepdims=True) * v_h).reshape(2, B, L, D)
    out = jnp.sum((yn + bonus) * g, axis=0)
    return (out @ w_o).astype(x.dtype)


def _grouped_experts(xt, expert, gate, w1, w3, w2):
    T, D = xt.shape
    A = T * MOE_TOPK
    n_blocks = -(-A // MOE_BLOCK) + MOE_EXPERTS
    flat_e = expert.reshape(-1)
    order = jnp.argsort(flat_e)
    e_sorted = flat_e[order]
    tok = order // MOE_TOPK
    counts = jnp.bincount(flat_e, length=MOE_EXPERTS)
    padded = (counts + MOE_BLOCK - 1) // MOE_BLOCK * MOE_BLOCK
    pad_end = jnp.cumsum(padded)
    pad_start = pad_end - padded
    start = jnp.cumsum(counts) - counts
    dest = pad_start[e_sorted] + jnp.arange(A) - start[e_sorted]
    rows = jnp.zeros((n_blocks * MOE_BLOCK, D), xt.dtype).at[dest].set(xt[tok])
    blk_e = jnp.minimum(jnp.searchsorted(pad_end, jnp.arange(n_blocks) * MOE_BLOCK, side='right'), MOE_EXPERTS - 1)

    def block(args):
        xb, e = args
        h = jax.nn.silu(xb @ w1[e]) * (xb @ w3[e])
        return h @ w2[e]

    out = lax.map(block, (rows.reshape(n_blocks, MOE_BLOCK, D), blk_e)).reshape(-1, D)
    contrib = out[dest] * gate.reshape(-1)[order][:, None].astype(out.dtype)
    return jnp.zeros_like(xt).at[tok].add(contrib)


def _hier_moe(x, w_group, w_expert, w1, w3, w2):
    B, L, D = x.shape
    xt = x.reshape(-1, D)
    T = xt.shape[0]
    gl = (xt @ w_group).astype(F32)
    g_top = jnp.argmax(gl, axis=-1)
    g_prob = jnp.take_along_axis(jax.nn.softmax(gl, -1), g_top[:, None], axis=1)
    el = (xt @ w_expert).astype(F32).reshape(T, MOE_GROUPS, MOE_PER_GROUP)
    el = jnp.take_along_axis(el, g_top[:, None, None], axis=1)[:, 0]
    e_w, e_i = lax.top_k(jax.nn.softmax(el, -1), MOE_TOPK)
    gate = g_prob * (e_w / jnp.sum(e_w, -1, keepdims=True))
    expert = g_top[:, None] * MOE_PER_GROUP + e_i
    return _grouped_experts(xt, expert, gate, w1, w3, w2).reshape(B, L, D)


def _trunk(x, p):
    L = x.shape[1]
    for layer in range(DEPTH):
        i = layer // 2
        if layer % 2 == 0:
            filt = _hyena_filters(L, p['hy_ffn_w1'][i], p['hy_ffn_b1'][i], p['hy_ffn_w2'][i], p['hy_ffn_b2'][i],
                                  p['hy_ffn_w3'][i], p['hy_sin_freq'][i], p['hy_decay'][i])
            m = _hyena_mla_layer(x, p['mix_w_in'][i], p['hy_conv_w'][i], p['hy_conv_b'][i], p['hy_skip'][i], filt,
                                 p['mla_q_norm'][i], p['mla_w_uq'][i], p['mla_kv_norm'][i], p['mla_w_ukv'][i],
                                 p['mix_w_out'][i])
        else:
            m = _rwkv7(x, p['rw_mu'][i], p['rw_w_r'][i], p['rw_w_k'][i], p['rw_w_v'][i], p['rw_w0'][i],
                       p['rw_w1'][i], p['rw_w2'][i], p['rw_a0'][i], p['rw_a1'][i], p['rw_a2'][i], p['rw_g1'][i],
                       p['rw_g2'][i], p['rw_k_k'][i], p['rw_k_a'][i], p['rw_r_k'][i], p['rw_gn_g'][i],
                       p['rw_gn_b'][i], p['rw_w_o'][i])
        x = _layer_norm(DN_ALPHA * x + m.astype(x.dtype), p['ln1_g'][layer], p['ln1_b'][layer])
        f = _hier_moe(x, p['moe_w_group'][layer], p['moe_w_expert'][layer], p['moe_w1'][layer],
                      p['moe_w3'][layer], p['moe_w2'][layer])
        x = _layer_norm(DN_ALPHA * x + f.astype(x.dtype), p['ln2_g'][layer], p['ln2_b'][layer])
    return x


def setup_inputs(seed: int = 0) -> dict:
    key = jax.random.key(seed)
    ks = iter(jax.random.split(key, 64))

    def nrm(shape, scale):
        return jax.random.normal(next(ks), shape, F32) * scale

    def uni(shape, lo, hi):
        return jax.random.uniform(next(ks), shape, F32, lo, hi)

    D, E, O = D_MODEL, N_EVEN, N_ODD
    hy_lo = -math.log(HY_DECAY_TARGET) / HY_SLOW_DECAY
    hy_hi = -math.log(HY_DECAY_TARGET) / HY_FAST_DECAY
    decay_base = jnp.linspace(hy_lo, hy_hi, HY_W, dtype=F32)
    return {
        'x_prompt': nrm((BATCH, SEQ, D), 1.0),
        'x_sample': nrm((DEC_BATCH, DEC_SEQ, D), 1.0),
        'ln1_g': 1.0 + nrm((DEPTH, D), 0.02),
        'ln1_b': nrm((DEPTH, D), 0.02),
        'ln2_g': 1.0 + nrm((DEPTH, D), 0.02),
        'ln2_b': nrm((DEPTH, D), 0.02),
        'mix_w_in': nrm((E, D, IN_WIDTH), D ** -0.5),
        'hy_conv_w': nrm((E, HY_SHORT, (HY_ORDER + 1) * HY_W), HY_SHORT ** -0.5),
        'hy_conv_b': nrm((E, (HY_ORDER + 1) * HY_W), 0.02),
        'hy_ffn_w1': nrm((E, HY_POS_DIM, HY_FILT_HID), HY_POS_DIM ** -0.5),
        'hy_ffn_b1': nrm((E, HY_FILT_HID), 0.02),
        'hy_ffn_w2': nrm((E, HY_FILT_HID, HY_FILT_HID), HY_FILT_HID ** -0.5),
        'hy_ffn_b2': nrm((E, HY_FILT_HID), 0.02),
        'hy_ffn_w3': nrm((E, HY_FILT_HID, 2 * HY_ORDER * HY_W), HY_FILT_HID ** -0.5),
        'hy_sin_freq': 1.0 + nrm((E, HY_FILT_HID), 0.1),
        'hy_decay': decay_base * (1.0 + nrm((E, 2, HY_ORDER, HY_W), 0.1)),
        'hy_skip': nrm((E, HY_ORDER, HY_W), 1.0),
        'mla_q_norm': 1.0 + nrm((E, MLA_Q_LORA), 0.02),
        'mla_w_uq': nrm((E, MLA_Q_LORA, MLA_HEADS * (MLA_NOPE + MLA_ROPE)), MLA_Q_LORA ** -0.5),
        'mla_kv_norm': 1.0 + nrm((E, MLA_KV_LORA), 0.02),
        'mla_w_ukv': nrm((E, MLA_KV_LORA, MLA_HEADS * (MLA_NOPE + MLA_V)), MLA_KV_LORA ** -0.5),
        'mix_w_out': nrm((E, MIX_OUT_W, D), DN_BETA * MIX_OUT_W ** -0.5),
        'rw_mu': uni((O, 2, 6, D), 0.0, 1.0),
        'rw_w_r': nrm((O, D, D), D ** -0.5),
        'rw_w_k': nrm((O, D, D), D ** -0.5),
        'rw_w_v': nrm((O, D, D), D ** -0.5),
        'rw_w0': uni((O, 2, D), -6.0, -1.0),
        'rw_w1': nrm((O, 2, D, RW_DECAY_LORA), 0.1 * D ** -0.5),
        'rw_w2': nrm((O, 2, RW_DECAY_LORA, D), 0.1 * RW_DECAY_LORA ** -0.5),
        'rw_a0': nrm((O, 2, D), 0.1),
        'rw_a1': nrm((O, 2, D, RW_AAA_LORA), 0.1 * D ** -0.5),
        'rw_a2': nrm((O, 2, RW_AAA_LORA, D), 0.1 * RW_AAA_LORA ** -0.5),
        'rw_g1': nrm((O, 2, D, RW_GATE_LORA), D ** -0.5),
        'rw_g2': nrm((O, 2, RW_GATE_LORA, D), RW_GATE_LORA ** -0.5),
        'rw_k_k': 0.85 + nrm((O, D), 0.02),
        'rw_k_a': 1.0 + nrm((O, D), 0.02),
        'rw_r_k': -0.04 + nrm((O, RW_HEADS, RW_HEAD), 0.02),
        'rw_gn_g': 1.0 + nrm((O, D), 0.02),
        'rw_gn_b': nrm((O, D), 0.02),
        'rw_w_o': nrm((O, D, D), DN_BETA * D ** -0.5),
        'moe_w_group': nrm((DEPTH, D, MOE_GROUPS), D ** -0.5),
        'moe_w_expert': nrm((DEPTH, D, MOE_EXPERTS), D ** -0.5),
        'moe_w1': nrm((DEPTH, MOE_EXPERTS, D, MOE_FF), D ** -0.5),
        'moe_w3': nrm((DEPTH, MOE_EXPERTS, D, MOE_FF), D ** -0.5),
        'moe_w2': nrm((DEPTH, MOE_EXPERTS, MOE_FF, D), DN_BETA * MOE_FF ** -0.5),
    }


def reference(x_prompt, x_sample, ln1_g, ln1_b, ln2_g, ln2_b, mix_w_in, hy_conv_w, hy_conv_b, hy_ffn_w1,
              hy_ffn_b1, hy_ffn_w2, hy_ffn_b2, hy_ffn_w3, hy_sin_freq, hy_decay, hy_skip, mla_q_norm, mla_w_uq,
              mla_kv_norm, mla_w_ukv, mix_w_out, rw_mu, rw_w_r, rw_w_k, rw_w_v, rw_w0, rw_w1, rw_w2, rw_a0,
              rw_a1, rw_a2, rw_g1, rw_g2, rw_k_k, rw_k_a, rw_r_k, rw_gn_g, rw_gn_b, rw_w_o, moe_w_group,
              moe_w_expert, moe_w1, moe_w3, moe_w2):
    p = dict(ln1_g=ln1_g, ln1_b=ln1_b, ln2_g=ln2_g, ln2_b=ln2_b, mix_w_in=mix_w_in, hy_conv_w=hy_conv_w,
             hy_conv_b=hy_conv_b, hy_ffn_w1=hy_ffn_w1, hy_ffn_b1=hy_ffn_b1, hy_ffn_w2=hy_ffn_w2,
             hy_ffn_b2=hy_ffn_b2, hy_ffn_w3=hy_ffn_w3, hy_sin_freq=hy_sin_freq, hy_decay=hy_decay,
             hy_skip=hy_skip, mla_q_norm=mla_q_norm, mla_w_uq=mla_w_uq, mla_kv_norm=mla_kv_norm,
             mla_w_ukv=mla_w_ukv, mix_w_out=mix_w_out, rw_mu=rw_mu, rw_w_r=rw_w_r, rw_w_k=rw_w_k,
             rw_w_v=rw_w_v, rw_w0=rw_w0, rw_w1=rw_w1, rw_w2=rw_w2, rw_a0=rw_a0, rw_a1=rw_a1, rw_a2=rw_a2,
             rw_g1=rw_g1, rw_g2=rw_g2, rw_k_k=rw_k_k, rw_k_a=rw_k_a, rw_r_k=rw_r_k, rw_gn_g=rw_gn_g,
             rw_gn_b=rw_gn_b, rw_w_o=rw_w_o, moe_w_group=moe_w_group, moe_w_expert=moe_w_expert,
             moe_w1=moe_w1, moe_w3=moe_w3, moe_w2=moe_w2)
    y_prompt = _trunk(x_prompt, p)
    y_sample = _trunk(x_sample, p)
    return (y_prompt, y_sample)
```

```python
import functools
import math

import jax
import jax.numpy as jnp
import numpy as np
from jax import lax
from jax.experimental import pallas as pl
from jax.experimental.pallas import tpu as pltpu

F32 = jnp.float32
BF16 = jnp.bfloat16

LANES = 128
VMEM_LIMIT_BYTES = 56 * 1024 * 1024

D_MODEL = 1024
DEPTH = 4
HY_W = D_MODEL // 2
HY_ORDER = 2
HY_BANDS = 16
MLA_HEADS = 8
MLA_NOPE = 64
MLA_ROPE = 32
MLA_V = 64
MLA_Q_LORA = 256
MLA_KV_LORA = 128
ROPE_THETA = 10000.0
RW_HEAD = 64
RW_HEADS = D_MODEL // RW_HEAD
RW_GN_EPS = 64e-5
MOE_GROUPS = 4
MOE_PER_GROUP = 8
MOE_EXPERTS = MOE_GROUPS * MOE_PER_GROUP
MOE_TOPK = 2
MOE_FF = 512
LN_EPS = 1e-5
RMS_EPS = 1e-6
DN_ALPHA = (2 * DEPTH) ** 0.25

RW_CHUNK = 64
HEADS_PER_SLAB = LANES // RW_HEAD


def _cparams(sem):
    return pltpu.CompilerParams(dimension_semantics=sem, vmem_limit_bytes=VMEM_LIMIT_BYTES)


def _bdot(a, b):
    return jnp.dot(a.astype(BF16), b.astype(BF16), preferred_element_type=F32)


def _bdot_nt(a, b):
    return lax.dot_general(a.astype(BF16), b.astype(BF16), (((1,), (1,)), ((), ())),
                           preferred_element_type=F32)


def _bdot_tn(a, b):
    return lax.dot_general(a.astype(BF16), b.astype(BF16), (((0,), (0,)), ((), ())),
                           preferred_element_type=F32)


def _rwkv_scan_kernel(r_ref, lw_ref, k_ref, v_ref, kk_ref, b_ref, g_ref, rk_ref, gng_ref, gnb_ref,
                      o_ref, s_ref, *, tb):
    C = RW_CHUNK
    z = pl.program_id(0)
    fwd = z == 0

    @pl.when(pl.program_id(3) == 0)
    def _():
        s_ref[...] = jnp.zeros_like(s_ref)

    lane = lax.broadcasted_iota(jnp.int32, (1, LANES), 1)
    m0 = lane < RW_HEAD
    sgn = jnp.where(fwd, 1, -1)
    ri = lax.broadcasted_iota(jnp.int32, (C, C), 0)
    ci = lax.broadcasted_iota(jnp.int32, (C, C), 1)
    tri_c = jnp.where((ci - ri) * sgn <= 0, 1.0, 0.0).astype(F32)
    r2 = lax.broadcasted_iota(jnp.int32, (2 * C, 2 * C), 0)
    c2 = lax.broadcasted_iota(jnp.int32, (2 * C, 2 * C), 1)
    head_r = jnp.where(r2 >= C, 1, 0)
    head_c = jnp.where(c2 >= C, 1, 0)
    lag = jnp.where(head_r == head_c, (c2 - r2) * sgn, 1)
    strict = lag < 0
    incl = lag <= 0
    rk = rk_ref[...]
    gng = gng_ref[...]
    gnb = gnb_ref[...]

    def stack(x):
        return jnp.concatenate([jnp.where(m0, x, 0.0), jnp.where(m0, 0.0, x)], axis=0)

    def head_sum(x):
        s0 = jnp.sum(jnp.where(m0, x, 0.0), axis=-1, keepdims=True)
        s1 = jnp.sum(jnp.where(m0, 0.0, x), axis=-1, keepdims=True)
        return jnp.where(m0, s0, s1)

    def chunk(j, carry):
        start = jnp.where(fwd, j * C, tb - (j + 1) * C)
        start = pl.multiple_of(start, C)
        sl = pl.ds(start, C)
        r = r_ref[sl, :]
        lw = lw_ref[sl, :]
        k = k_ref[sl, :]
        v = v_ref[sl, :]
        kk = kk_ref[sl, :]
        b = b_ref[sl, :]
        g = g_ref[sl, :]
        S = s_ref[...]

        cl = jnp.dot(tri_c, lw, precision=lax.Precision.HIGHEST, preferred_element_type=F32)
        cl_last = jnp.where(fwd, cl[C - 1:C, :], cl[0:1, :])
        eneg = jnp.exp(-cl)
        epos = jnp.exp(cl)
        ehat = jnp.exp(cl_last - cl)
        at = -kk * jnp.exp(cl - lw)
        la_lr = jnp.concatenate([stack(at), stack(r * epos)], axis=0).astype(BF16)
        rb_rk = jnp.concatenate([stack(b * eneg), stack(k * eneg)], axis=0).astype(BF16)
        v2 = stack(v)

        sc = _bdot_nt(la_lr, rb_rk)
        mab = jnp.where(strict, sc[:2 * C, :2 * C], 0.0)
        mak = jnp.where(strict, sc[:2 * C, 2 * C:], 0.0)
        nrb = jnp.where(incl, sc[2 * C:, :2 * C], 0.0)
        nrk = jnp.where(incl, sc[2 * C:, 2 * C:], 0.0)
        hs = _bdot_nt(la_lr, S)
        x = hs[:2 * C] + _bdot(mak, v2)
        mp = mab
        x = x + _bdot(mp, x)
        for _ in range(int(math.log2(C)) - 1):
            mp = _bdot(mp, mp)
            x = x + _bdot(mp, x)
        uv = jnp.concatenate([x, v2], axis=0)
        y2 = hs[2 * C:] + _bdot(jnp.concatenate([nrb, nrk], axis=1), uv)
        y = y2[:C] + y2[C:]

        bh_kh = jnp.concatenate([stack(b * ehat), stack(k * ehat)], axis=0)
        s_ref[...] = S * jnp.exp(cl_last) + _bdot_tn(uv, bh_kh)

        mean = head_sum(y) * (1.0 / RW_HEAD)
        d = y - mean
        var = head_sum(d * d) * (1.0 / RW_HEAD)
        yn = d * lax.rsqrt(var + RW_GN_EPS) * gng + gnb
        bonus = head_sum(r * k * rk) * v
        o_ref[sl, :] = (yn + bonus) * g
        return carry

    lax.fori_loop(0, tb // C, chunk, 0)


def _rwkv_scan(r, lw, k, v, kk, b, g, r_k, gn_g, gn_b, *, tb=512):
    Z, B, L, D = r.shape
    tb = min(tb, L)
    nblk = L // tb
    nslab = D // LANES

    def amap(z, bb, h, i):
        return (z, bb, jnp.where(z == 0, i, nblk - 1 - i), h)

    act = pl.BlockSpec((None, None, tb, LANES), amap)
    par = pl.BlockSpec((1, LANES), lambda z, bb, h, i: (0, h))
    return pl.pallas_call(
        functools.partial(_rwkv_scan_kernel, tb=tb),
        grid=(Z, B, nslab, nblk),
        in_specs=[act] * 7 + [par] * 3,
        out_specs=act,
        out_shape=jax.ShapeDtypeStruct((Z, B, L, D), F32),
        scratch_shapes=[pltpu.VMEM((LANES, LANES), F32)],
        compiler_params=_cparams(("parallel", "parallel", "parallel", "arbitrary")),
        name="rwkv_scan",
    )(r, lw, k, v, kk, b, g, r_k.reshape(1, D), gn_g.reshape(1, D), gn_b.reshape(1, D))


HALO_ROWS = 8


def _halo_specs(tm, width, seq_len, lead):
    per = tm // HALO_ROWS
    last = seq_len // HALO_ROWS - 1

    def main(*g):
        return (g[lead], g[lead + 1], 0)

    def prev(*g):
        return (g[lead], jnp.maximum(g[lead + 1] * per - 1, 0), 0)

    def nxt(*g):
        return (g[lead], jnp.minimum((g[lead + 1] + 1) * per, last), 0)

    return (pl.BlockSpec((None, tm, width), main), pl.BlockSpec((None, HALO_ROWS, width), prev),
            pl.BlockSpec((None, HALO_ROWS, width), nxt))


def _shifted(x, prev_ref, next_ref, i, n_tiles):
    tm = x.shape[0]
    prev_row = jnp.where(i == 0, 0.0, prev_ref[HALO_ROWS - 1:HALO_ROWS, :])
    next_row = jnp.where(i == n_tiles - 1, 0.0, next_ref[0:1, :])
    row = lax.broadcasted_iota(jnp.int32, (tm, 1), 0)
    x_prev = jnp.where(row == 0, prev_row, pltpu.roll(x, 1, axis=0))
    x_next = jnp.where(row == tm - 1, next_row, pltpu.roll(x, tm - 1, axis=0))
    return x_prev, x_next


def _layer_norm_rows(y, g, b):
    mu = jnp.mean(y, axis=-1, keepdims=True)
    d = y - mu
    var = jnp.mean(d * d, axis=-1, keepdims=True)
    return d * lax.rsqrt(var + LN_EPS) * g + b


RW_LORA_PAD = 128
RW_GATE_PAD = 256
RW_DECAY_SCALE = math.exp(-0.5)


def _rwkv_proj_kernel(x_ref, xp_ref, xn_ref, mu_ref, wr_ref, wk_ref, wv_ref, w0_ref, w1_ref, w2_ref,
                      a0_ref, a1_ref, a2_ref, g1_ref, g2_ref, kk_ref, ka_ref,
                      r_out, lw_out, k_out, v_out, kk_out, b_out, g_out):
    z = pl.program_id(0)
    i = pl.program_id(2)
    x = x_ref[...]
    x_prev, x_next = _shifted(x, xp_ref, xn_ref, i, pl.num_programs(2))
    xx = jnp.where(z == 0, x_prev, x_next) - x
    mu = mu_ref[...]

    def mix(j):
        return (x + xx * mu[j:j + 1, :]).astype(BF16)

    r = jnp.dot(mix(0), wr_ref[...], preferred_element_type=F32)
    k = jnp.dot(mix(2), wk_ref[...], preferred_element_type=F32)
    v = jnp.dot(mix(3), wv_ref[...], preferred_element_type=F32)
    lw = w0_ref[...] + _bdot(jnp.tanh(jnp.dot(mix(1), w1_ref[...], preferred_element_type=F32)), w2_ref[...])
    logw = -jax.nn.sigmoid(lw) * RW_DECAY_SCALE
    a = jax.nn.sigmoid(a0_ref[...] + _bdot(jnp.dot(mix(4), a1_ref[...], preferred_element_type=F32), a2_ref[...]))
    g = _bdot(jax.nn.sigmoid(jnp.dot(mix(5), g1_ref[...], preferred_element_type=F32)), g2_ref[...])

    kkr = k * kk_ref[...]
    lane = lax.broadcasted_iota(jnp.int32, (1, LANES), 1)
    m0 = lane < RW_HEAD
    slabs = []
    for s in range(x.shape[1] // LANES):
        q = kkr[:, s * LANES:(s + 1) * LANES]
        q2 = q * q
        n0 = jnp.sum(jnp.where(m0, q2, 0.0), axis=-1, keepdims=True)
        n1 = jnp.sum(jnp.where(m0, 0.0, q2), axis=-1, keepdims=True)
        nrm = jnp.maximum(jnp.sqrt(jnp.where(m0, n0, n1)), 1e-12)
        slabs.append(q / nrm)
    kk = jnp.concatenate(slabs, axis=1)

    r_out[...] = r
    lw_out[...] = logw
    k_out[...] = k * (1.0 + (a - 1.0) * ka_ref[...])
    v_out[...] = v
    kk_out[...] = kk
    b_out[...] = kk * a
    g_out[...] = g


def _pad_axis(w, axis, size):
    pad = [(0, 0)] * w.ndim
    pad[axis] = (0, size - w.shape[axis])
    return jnp.pad(w, pad)


def _rwkv_proj(x, p, *, tm=256):
    B, L, D = x.shape
    tm = min(tm, L)
    main, prev, nxt = _halo_specs(tm, D, L, lead=1)
    full2 = lambda shape: pl.BlockSpec(shape, lambda z, b, i: (0, 0))
    perz = lambda shape: pl.BlockSpec((None,) + shape, lambda z, b, i: (z, 0, 0))
    out_spec = pl.BlockSpec((None, None, tm, D), lambda z, b, i: (z, b, i, 0))
    w1 = _pad_axis(p['w1'], 2, RW_LORA_PAD).astype(BF16)
    w2 = _pad_axis(p['w2'], 1, RW_LORA_PAD).astype(BF16)
    a1 = _pad_axis(p['a1'], 2, RW_LORA_PAD).astype(BF16)
    a2 = _pad_axis(p['a2'], 1, RW_LORA_PAD).astype(BF16)
    g1 = _pad_axis(p['g1'], 2, RW_GATE_PAD).astype(BF16)
    g2 = _pad_axis(p['g2'], 1, RW_GATE_PAD).astype(BF16)
    outs = pl.pallas_call(
        _rwkv_proj_kernel,
        grid=(2, B, L // tm),
        in_specs=[main, prev, nxt, perz((6, D)), full2((D, D)), full2((D, D)), full2((D, D)),
                  perz((1, D)), perz((D, RW_LORA_PAD)), perz((RW_LORA_PAD, D)),
                  perz((1, D)), perz((D, RW_LORA_PAD)), perz((RW_LORA_PAD, D)),
                  perz((D, RW_GATE_PAD)), perz((RW_GATE_PAD, D)), full2((1, D)), full2((1, D))],
        out_specs=[out_spec] * 7,
        out_shape=[jax.ShapeDtypeStruct((2, B, L, D), F32)] * 7,
        compiler_params=_cparams(("parallel", "parallel", "parallel")),
        name="rwkv_proj",
    )(x, x, x, p['mu'], p['w_r'].astype(BF16), p['w_k'].astype(BF16), p['w_v'].astype(BF16),
      p['w0'].reshape(2, 1, D), w1, w2, p['a0'].reshape(2, 1, D), a1, a2, g1, g2,
      p['k_k'].reshape(1, D), p['k_a'].reshape(1, D))
    return outs


def _rw_out_ln_kernel(o_ref, w_ref, res_ref, g_ref, b_ref, out_ref):
    s = (o_ref[0] + o_ref[1]).astype(BF16)
    m = jnp.dot(s, w_ref[...], preferred_element_type=F32)
    out_ref[...] = _layer_norm_rows(DN_ALPHA * res_ref[...] + m, g_ref[...], b_ref[...])


def _rw_out_ln(o, w_o, res, ln_g, ln_b, *, tm=512):
    _, T, D = o.shape
    tm = min(tm, T)
    row = pl.BlockSpec((tm, D), lambda i: (i, 0))
    vec = pl.BlockSpec((1, D), lambda i: (0, 0))
    return pl.pallas_call(
        _rw_out_ln_kernel,
        grid=(T // tm,),
        in_specs=[pl.BlockSpec((2, tm, D), lambda i: (0, i, 0)), pl.BlockSpec((D, D), lambda i: (0, 0)),
                  row, vec, vec],
        out_specs=row,
        out_shape=jax.ShapeDtypeStruct((T, D), F32),
        compiler_params=_cparams(("parallel",)),
        name="rw_out_ln",
    )(o, w_o.astype(BF16), res, ln_g.reshape(1, D), ln_b.reshape(1, D))


def _rwkv_layer(x, p, ln_g, ln_b):
    B, L, D = x.shape
    r, lw, k, v, kk, b, g = _rwkv_proj(x, p)
    o = _rwkv_scan(r, lw, k, v, kk, b, g, p['r_k'].reshape(D), p['gn_g'], p['gn_b'])
    return _rw_out_ln(o.reshape(2, B * L, D), p['w_o'], x.reshape(B * L, D), ln_g, ln_b).reshape(B, L, D)


HY_U_W = (HY_ORDER + 1) * HY_W
MLA_HEAD_PAD = LANES
MLA_QK_W = MLA_HEADS * MLA_HEAD_PAD
MLA_V_W = MLA_HEADS * MLA_V
MLA_SCALE = (MLA_NOPE + MLA_ROPE) ** -0.5
HALF_ROPE = MLA_ROPE // 2


def _rms_rows(c, g):
    return c * lax.rsqrt(jnp.mean(c * c, axis=-1, keepdims=True) + RMS_EPS) * g


def _in_proj_kernel(x_ref, wu_ref, ws_ref, wq_ref, wkv_ref, qn_ref, kvn_ref, cos_ref, sin_ref,
                    u_out, q_out, k_out, v_out):
    xb = x_ref[...].astype(BF16)
    u_out[...] = jnp.dot(xb, wu_ref[...], preferred_element_type=F32)
    small = jnp.dot(xb, ws_ref[...], preferred_element_type=F32)
    c_q = small[:, :MLA_Q_LORA]
    c_kv = small[:, MLA_Q_LORA:MLA_Q_LORA + MLA_KV_LORA]
    kpe_a = small[:, MLA_Q_LORA + MLA_KV_LORA:MLA_Q_LORA + MLA_KV_LORA + LANES]
    kpe_b = small[:, MLA_Q_LORA + MLA_KV_LORA + LANES:]
    cos = cos_ref[...]
    sin = sin_ref[...]
    cos_h = jnp.concatenate([cos] * MLA_HEADS, axis=1)
    sin_h = jnp.concatenate([sin] * MLA_HEADS, axis=1)

    qq = _bdot(_rms_rows(c_q, qn_ref[...]), wq_ref[...])
    q_out[...] = (qq[:, :MLA_QK_W] * cos_h + qq[:, MLA_QK_W:] * sin_h).astype(BF16)

    kv = _bdot(_rms_rows(c_kv, kvn_ref[...]), wkv_ref[...])
    kpe = kpe_a * cos + kpe_b * sin
    k_out[...] = (kv[:, :MLA_QK_W] + jnp.concatenate([kpe] * MLA_HEADS, axis=1)).astype(BF16)
    v_out[...] = kv[:, MLA_QK_W:].astype(BF16)


def _rope_head_tables(L):
    inv = ROPE_THETA ** (-jnp.arange(0, MLA_ROPE, 2, dtype=F32) / MLA_ROPE)
    ang = jnp.arange(L, dtype=F32)[:, None] * inv[None, :]
    cos, sin = jnp.cos(ang), jnp.sin(ang)
    ones = jnp.ones((L, MLA_NOPE), F32)
    zn = jnp.zeros((L, MLA_NOPE), F32)
    zp = jnp.zeros((L, MLA_HEAD_PAD - MLA_NOPE - MLA_ROPE), F32)
    return (jnp.concatenate([ones, cos, cos, zp], axis=1), jnp.concatenate([zn, sin, sin, zp], axis=1))


def _mla_weights(w_in, w_uq, w_ukv):
    s1 = HY_U_W
    s2 = s1 + MLA_Q_LORA
    s3 = s2 + MLA_KV_LORA
    kx1, kx2 = w_in[:, s3:s3 + HALF_ROPE], w_in[:, s3 + HALF_ROPE:]
    zl = jnp.zeros((w_in.shape[0], MLA_NOPE), F32)
    zr = jnp.zeros((w_in.shape[0], MLA_HEAD_PAD - MLA_NOPE - MLA_ROPE), F32)
    kpe_a = jnp.concatenate([zl, kx1, kx2, zr], axis=1)
    kpe_b = jnp.concatenate([zl, -kx2, kx1, zr], axis=1)
    w_small = jnp.concatenate([w_in[:, s1:s3], kpe_a, kpe_b], axis=1)
    q3 = w_uq.reshape(MLA_Q_LORA, MLA_HEADS, MLA_NOPE + MLA_ROPE) * MLA_SCALE
    nope, x1, x2 = q3[..., :MLA_NOPE], q3[..., MLA_NOPE:MLA_NOPE + HALF_ROPE], q3[..., MLA_NOPE + HALF_ROPE:]
    zq = jnp.zeros((MLA_Q_LORA, MLA_HEADS, MLA_HEAD_PAD - MLA_NOPE - MLA_ROPE), F32)
    wqa = jnp.concatenate([nope, x1, x2, zq], axis=-1).reshape(MLA_Q_LORA, MLA_QK_W)
    wqb = jnp.concatenate([jnp.zeros_like(nope), -x2, x1, zq], axis=-1).reshape(MLA_Q_LORA, MLA_QK_W)
    kv3 = w_ukv.reshape(MLA_KV_LORA, MLA_HEADS, MLA_NOPE + MLA_V)
    wkn = jnp.concatenate([kv3[..., :MLA_NOPE], jnp.zeros((MLA_KV_LORA, MLA_HEADS, MLA_HEAD_PAD - MLA_NOPE), F32)],
                          axis=-1).reshape(MLA_KV_LORA, MLA_QK_W)
    wv = kv3[..., MLA_NOPE:].reshape(MLA_KV_LORA, MLA_V_W)
    return (w_in[:, :s1].astype(BF16), w_small.astype(BF16), jnp.concatenate([wqa, wqb], axis=1).astype(BF16),
            jnp.concatenate([wkn, wv], axis=1).astype(BF16))


def _in_proj(x, w_in, w_uq, w_ukv, q_norm, kv_norm, *, tm=256):
    B, L, D = x.shape
    tm = min(tm, L)
    wu, ws, wq, wkv = _mla_weights(w_in, w_uq, w_ukv)
    cos, sin = _rope_head_tables(L)
    row = lambda w: pl.BlockSpec((None, tm, w), lambda b, i: (b, i, 0))
    full = lambda a: pl.BlockSpec(a.shape, lambda b, i: (0, 0))
    pos = pl.BlockSpec((tm, LANES), lambda b, i: (i, 0))
    qn = q_norm.reshape(1, -1)
    kvn = kv_norm.reshape(1, -1)
    return pl.pallas_call(
        _in_proj_kernel,
        grid=(B, L // tm),
        in_specs=[row(D), full(wu), full(ws), full(wq), full(wkv), full(qn), full(kvn), pos, pos],
        out_specs=[row(HY_U_W), row(MLA_QK_W), row(MLA_QK_W), row(MLA_V_W)],
        out_shape=[jax.ShapeDtypeStruct((B, L, HY_U_W), F32), jax.ShapeDtypeStruct((B, L, MLA_QK_W), BF16),
                   jax.ShapeDtypeStruct((B, L, MLA_QK_W), BF16), jax.ShapeDtypeStruct((B, L, MLA_V_W), BF16)],
        compiler_params=_cparams(("parallel", "parallel")),
        name="in_proj",
    )(x, wu, ws, wq, wkv, qn, kvn, cos, sin)


MLA_PAIR = 2


def _attn_kernel(q_ref, k_ref, v_ref, o_ref, m_ref, l_ref, acc_ref):
    ki = pl.program_id(3)

    @pl.when(ki == 0)
    def _():
        m_ref[...] = jnp.full_like(m_ref, -jnp.inf)
        l_ref[...] = jnp.zeros_like(l_ref)
        acc_ref[...] = jnp.zeros_like(acc_ref)

    v = v_ref[...]
    for h in range(MLA_PAIR):
        q = q_ref[:, h * MLA_HEAD_PAD:(h + 1) * MLA_HEAD_PAD]
        k = k_ref[:, h * MLA_HEAD_PAD:(h + 1) * MLA_HEAD_PAD]
        s = lax.dot_general(q, k, (((1,), (1,)), ((), ())), preferred_element_type=F32)
        m_old = m_ref[h]
        m_new = jnp.maximum(m_old, jnp.max(s, axis=-1, keepdims=True))
        alpha = jnp.exp(m_old - m_new)
        p = jnp.exp(s - m_new[:, 0:1])
        l_ref[h] = alpha * l_ref[h] + jnp.sum(p, axis=-1, keepdims=True)
        acc_ref[h] = alpha * acc_ref[h] + jnp.dot(p.astype(BF16), v, preferred_element_type=F32)
        m_ref[h] = m_new

    @pl.when(ki == pl.num_programs(3) - 1)
    def _():
        lane = lax.broadcasted_iota(jnp.int32, (1, LANES), 1)
        o0 = acc_ref[0] / l_ref[0]
        o1 = acc_ref[1] / l_ref[1]
        o_ref[...] = jnp.where(lane < MLA_V, o0, o1)


def _attention(q, k, v, *, tq=512, tk=512):
    B, L, _ = q.shape
    tq, tk = min(tq, L), min(tk, L)
    pairs = MLA_HEADS // MLA_PAIR
    return pl.pallas_call(
        _attn_kernel,
        grid=(B, pairs, L // tq, L // tk),
        in_specs=[pl.BlockSpec((None, tq, MLA_PAIR * MLA_HEAD_PAD), lambda b, h, qi, ki: (b, qi, h)),
                  pl.BlockSpec((None, tk, MLA_PAIR * MLA_HEAD_PAD), lambda b, h, qi, ki: (b, ki, h)),
                  pl.BlockSpec((None, tk, LANES), lambda b, h, qi, ki: (b, ki, h))],
        out_specs=pl.BlockSpec((None, tq, LANES), lambda b, h, qi, ki: (b, qi, h)),
        out_shape=jax.ShapeDtypeStruct((B, L, MLA_V_W), F32),
        scratch_shapes=[pltpu.VMEM((MLA_PAIR, tq, LANES), F32), pltpu.VMEM((MLA_PAIR, tq, LANES), F32),
                        pltpu.VMEM((MLA_PAIR, tq, LANES), F32)],
        compiler_params=_cparams(("parallel", "parallel", "parallel", "arbitrary")),
        name="attention",
    )(q, k, v)


def _split3_rows(x):
    hi = x.astype(BF16)
    lo = (x - hi.astype(F32)).astype(BF16)
    return jnp.concatenate([hi, lo, hi], axis=0)


def _split3_cols(m):
    hi = m.astype(BF16)
    lo = (m - hi.astype(F32)).astype(BF16)
    return jnp.concatenate([hi, hi, lo], axis=-1)


def _fft_factor(n):
    n1 = 1 << ((n.bit_length() - 1 + 1) // 2)
    return n1, n // n1


def _dft_tables(L):
    N = 2 * L
    N1, N2 = _fft_factor(N)
    h = N1 // 2
    k1 = jnp.arange(N1, dtype=jnp.int32)
    ang1 = (-2.0 * math.pi / N1) * ((k1[:, None] * k1[None, :]) % N1).astype(F32)
    f1r, f1i = jnp.cos(ang1), jnp.sin(ang1)
    lead_pair = jnp.concatenate([jnp.concatenate([f1r[:, :h], -f1i[:, :h]], axis=1),
                                 jnp.concatenate([f1i[:, :h], f1r[:, :h]], axis=1)], axis=0)
    lead_real = jnp.concatenate([f1r, f1i], axis=0)
    f1rt, f1it = f1r.T[:h], f1i.T[:h]
    lead_inv = jnp.concatenate([jnp.concatenate([f1rt, f1it], axis=1),
                                jnp.concatenate([-f1it, f1rt], axis=1)], axis=0) * (1.0 / N)
    n2 = jnp.arange(N2, dtype=jnp.int32)
    freq = k1[:, None, None] + N1 * n2[None, :, None]
    idx = (freq * n2[None, None, :]) % N
    ang2 = (-2.0 * math.pi / N) * idx.astype(F32)
    gr, gi = jnp.cos(ang2), jnp.sin(ang2)
    inner_fwd = jnp.concatenate([jnp.concatenate([gr, -gi], axis=2),
                                 jnp.concatenate([gi, gr], axis=2)], axis=1)
    grt, git = jnp.swapaxes(gr, 1, 2), jnp.swapaxes(gi, 1, 2)
    inner_inv = jnp.concatenate([jnp.concatenate([grt, git], axis=2),
                                 jnp.concatenate([-git, grt], axis=2)], axis=1)
    return dict(N1=N1, N2=N2, lead_pair=_split3_cols(lead_pair), lead_real=_split3_cols(lead_real),
                lead_inv=_split3_cols(lead_inv), inner_fwd=_split3_cols(inner_fwd),
                inner_inv=_split3_cols(inner_inv))


FFT_LANE_TILE = 1024


def _dft_lead_kernel(m_ref, x_ref, o_ref):
    k = m_ref.shape[1] // 3
    x = x_ref[...].reshape(k, x_ref.shape[-1])
    o_ref[...] = jnp.dot(m_ref[...], _split3_rows(x), preferred_element_type=F32).reshape(o_ref.shape)


def _dft_lead_gate_kernel(m_ref, x_ref, z_ref, g_ref, sk_ref, o_ref):
    k = m_ref.shape[1] // 3
    x = x_ref[...].reshape(k, x_ref.shape[-1])
    conv = jnp.dot(m_ref[...], _split3_rows(x), preferred_element_type=F32).reshape(o_ref.shape)
    o_ref[...] = g_ref[...] * (conv + sk_ref[...] * z_ref[...])


def _dft_inner_conv_kernel(mf_ref, mi_ref, a_ref, kf_ref, o_ref):
    n2 = a_ref.shape[1]
    a = a_ref[...].reshape(2 * n2, a_ref.shape[-1])
    x = jnp.dot(mf_ref[...], _split3_rows(a), preferred_element_type=F32)
    xr, xi = x[:n2], x[n2:]
    kr, ki = kf_ref[0], kf_ref[1]
    y = jnp.concatenate([xr * kr - xi * ki, xr * ki + xi * kr], axis=0)
    o_ref[...] = jnp.dot(mi_ref[...], _split3_rows(y), preferred_element_type=F32).reshape(o_ref.shape)


def _dft_inner_filter_kernel(mf_ref, a_ref, s_ref, o_ref):
    n2 = a_ref.shape[1]
    a = a_ref[...].reshape(2 * n2, a_ref.shape[-1])
    x = jnp.dot(mf_ref[...], _split3_rows(a), preferred_element_type=F32) * s_ref[...]
    o_ref[...] = x.reshape(o_ref.shape)


def _filter_spectrum(kfilt, inv_norm, tabs):
    N, Cf = kfilt.shape
    N1, N2 = tabs['N1'], tabs['N2']
    W = N2 * Cf
    wc = min(FFT_LANE_TILE, W)
    a = pl.pallas_call(
        _dft_lead_kernel,
        grid=(W // wc,),
        in_specs=[pl.BlockSpec((2 * N1, 3 * N1), lambda j: (0, 0)), pl.BlockSpec((N1, wc), lambda j: (0, j))],
        out_specs=pl.BlockSpec((2, N1, wc), lambda j: (0, 0, j)),
        out_shape=jax.ShapeDtypeStruct((2, N1, W), F32),
        compiler_params=_cparams(("parallel",)),
        name="hyena_filter_dft_lead",
    )(tabs['lead_real'], kfilt.reshape(N1, W))
    return pl.pallas_call(
        _dft_inner_filter_kernel,
        grid=(N1,),
        in_specs=[pl.BlockSpec((None, 2 * N2, 6 * N2), lambda k1: (k1, 0, 0)),
                  pl.BlockSpec((2, None, N2, Cf), lambda k1: (0, k1, 0, 0)),
                  pl.BlockSpec((1, Cf), lambda k1: (0, 0))],
        out_specs=pl.BlockSpec((2, None, N2, Cf), lambda k1: (0, k1, 0, 0)),
        out_shape=jax.ShapeDtypeStruct((2, N1, N2, Cf), F32),
        compiler_params=_cparams(("parallel",)),
        name="hyena_filter_dft_inner",
    )(tabs['inner_fwd'], a.reshape(2, N1, N2, Cf), inv_norm)


def _long_conv_gate(z, gate, skip, kspec, order, tabs):
    B, L, C = z.shape
    N1, N2 = tabs['N1'], tabs['N2']
    h = N1 // 2
    P = B // 2
    W = N2 * C
    wc = min(FFT_LANE_TILE, W)
    zv = z.reshape(B, h, W)
    a = pl.pallas_call(
        _dft_lead_kernel,
        grid=(P, W // wc),
        in_specs=[pl.BlockSpec((2 * N1, 3 * N1), lambda p, j: (0, 0)),
                  pl.BlockSpec((2, h, wc), lambda p, j: (p, 0, j))],
        out_specs=pl.BlockSpec((None, 2, N1, wc), lambda p, j: (p, 0, 0, j)),
        out_shape=jax.ShapeDtypeStruct((P, 2, N1, W), F32),
        compiler_params=_cparams(("parallel", "parallel")),
        name="hyena_dft_lead",
    )(tabs['lead_pair'], zv)
    inner = pl.BlockSpec((None, 2, None, N2, C), lambda p, k1: (p, 0, k1, 0, 0))
    mat = pl.BlockSpec((None, 2 * N2, 6 * N2), lambda p, k1: (k1, 0, 0))
    bm = pl.pallas_call(
        _dft_inner_conv_kernel,
        grid=(P, N1),
        in_specs=[mat, mat, inner, pl.BlockSpec((2, None, N2, C), lambda p, k1: (0, k1, 0, order))],
        out_specs=inner,
        out_shape=jax.ShapeDtypeStruct((P, 2, N1, N2, C), F32),
        compiler_params=_cparams(("parallel", "parallel")),
        name="hyena_dft_inner",
    )(tabs['inner_fwd'], tabs['inner_inv'], a.reshape(P, 2, N1, N2, C), kspec)
    pair = pl.BlockSpec((2, h, wc), lambda p, j: (p, 0, j))
    sk = jnp.tile(skip.reshape(1, C), (1, wc // C))
    out = pl.pallas_call(
        _dft_lead_gate_kernel,
        grid=(P, W // wc),
        in_specs=[pl.BlockSpec((N1, 6 * N1), lambda p, j: (0, 0)),
                  pl.BlockSpec((None, 2, N1, wc), lambda p, j: (p, 0, 0, j)), pair, pair,
                  pl.BlockSpec((1, wc), lambda p, j: (0, 0))],
        out_specs=pair,
        out_shape=jax.ShapeDtypeStruct((B, h, W), F32),
        compiler_params=_cparams(("parallel", "parallel")),
        name="hyena_dft_lead_inv",
    )(tabs['lead_inv'], bm.reshape(P, 2, N1, W), zv, gate.reshape(B, h, W), sk)
    return out.reshape(B, L, C)


def _short_conv_kernel(u_ref, up_ref, un_ref, cw_ref, cb_ref, z_out, x1_out, x2_out):
    i = pl.program_id(1)
    u = u_ref[...]
    u_prev, u_next = _shifted(u, up_ref, un_ref, i, pl.num_programs(1))
    cw = cw_ref[...]
    y = u_prev * cw[0:1, :] + u * cw[1:2, :] + u_next * cw[2:3, :] + cb_ref[...]
    z_out[...] = y[:, :HY_W]
    x1_out[...] = y[:, HY_W:2 * HY_W]
    x2_out[...] = y[:, 2 * HY_W:]


def _short_conv(u, conv_w, conv_b, *, tm=512):
    B, L, W = u.shape
    tm = min(tm, L)
    main, prev, nxt = _halo_specs(tm, W, L, lead=0)
    part = pl.BlockSpec((None, tm, HY_W), lambda b, i: (b, i, 0))
    return pl.pallas_call(
        _short_conv_kernel,
        grid=(B, L // tm),
        in_specs=[main, prev, nxt, pl.BlockSpec((3, W), lambda b, i: (0, 0)), pl.BlockSpec((1, W), lambda b, i: (0, 0))],
        out_specs=[part] * 3,
        out_shape=[jax.ShapeDtypeStruct((B, L, HY_W), F32)] * 3,
        compiler_params=_cparams(("parallel", "parallel")),
        name="hyena_short_conv",
    )(u, u, u, conv_w, conv_b.reshape(1, W))


HY_FILT_HID = 64
HY_FILT_OUT = 2 * HY_ORDER * HY_W
TWO_PI = 2.0 * math.pi


def _hdot(a, b):
    return jnp.dot(a, b, precision=lax.Precision.HIGHEST, preferred_element_type=F32)


def _filter_mlp_kernel(bands_ref, w1t_ref, w1s_ref, w1c_ref, b1_ref, w2_ref, b2_ref, w3_ref, fr_ref, dec_ref,
                       h_out, abs_out, *, seq_len):
    i = pl.program_id(0)
    tm = h_out.shape[0]
    row = i * tm + lax.broadcasted_iota(jnp.int32, (tm, 1), 0)
    t = row.astype(F32) / seq_len
    ph = (TWO_PI * t) * bands_ref[...]
    fr = fr_ref[...]
    pre1 = t * w1t_ref[...] + _hdot(jnp.sin(ph), w1s_ref[...]) + _hdot(jnp.cos(ph), w1c_ref[...]) + b1_ref[...]
    h1 = jnp.sin(fr * pre1)
    h2 = jnp.sin(fr * (_hdot(h1, w2_ref[...]) + b2_ref[...]))
    hf = _hdot(h2, w3_ref[...]) * jnp.exp(-t * jnp.abs(dec_ref[...]))
    h_out[...] = hf

    @pl.when(i == 0)
    def _():
        abs_out[...] = jnp.zeros_like(abs_out)

    col = lax.broadcasted_iota(jnp.int32, (1, HY_FILT_OUT), 1)
    used = jnp.where(jnp.logical_and(row == 0, col >= HY_FILT_OUT // 2), 0.0, jnp.abs(hf))
    abs_out[...] += jnp.sum(used, axis=0, keepdims=True)


def _hyena_filter_spectrum(L, w1, b1, w2, b2, w3, freq, decay, tabs, *, tm=256):
    tm = min(tm, L)
    bands = _pad_axis(jnp.linspace(1e-4, HY_BANDS - 1, HY_BANDS, dtype=F32).reshape(1, HY_BANDS), 1, LANES)
    w1p = _pad_axis(w1, 1, LANES)
    w1t = w1p[0:1]
    w1s = _pad_axis(w1p[1:1 + HY_BANDS], 0, LANES)
    w1c = _pad_axis(w1p[1 + HY_BANDS:], 0, LANES)
    pad_vec = lambda v: _pad_axis(v.reshape(1, -1), 1, LANES)
    w2p = _pad_axis(_pad_axis(w2, 0, LANES), 1, LANES)
    w3p = _pad_axis(w3, 0, LANES)
    full = lambda a: pl.BlockSpec(a.shape, lambda i: (0, 0))
    args = (bands, w1t, w1s, w1c, pad_vec(b1), w2p, pad_vec(b2), w3p, pad_vec(freq), decay.reshape(1, HY_FILT_OUT))
    hf, abs_sum = pl.pallas_call(
        functools.partial(_filter_mlp_kernel, seq_len=L),
        grid=(L // tm,),
        in_specs=[full(a) for a in args],
        out_specs=[pl.BlockSpec((tm, HY_FILT_OUT), lambda i: (i, 0)), pl.BlockSpec((1, HY_FILT_OUT), lambda i: (0, 0))],
        out_shape=[jax.ShapeDtypeStruct((L, HY_FILT_OUT), F32), jax.ShapeDtypeStruct((1, HY_FILT_OUT), F32)],
        compiler_params=_cparams(("arbitrary",)),
        name="hyena_filter_mlp",
    )(*args)
    half = HY_FILT_OUT // 2
    kfilt = jnp.concatenate([hf[:, :half], jnp.zeros((1, half), F32), jnp.flip(hf[1:, half:], axis=0)], axis=0)
    inv_norm = 1.0 / (abs_sum[:, :half] + abs_sum[:, half:] + 1e-6)
    return _filter_spectrum(kfilt, inv_norm, tabs)


def _hyena(u, conv_w, conv_b, skip, kspec, tabs):
    z, x1, x2 = _short_conv(u, conv_w, conv_b)
    z = _long_conv_gate(z, x1, skip[0], kspec, 0, tabs)
    return _long_conv_gate(z, x2, skip[1], kspec, 1, tabs)


def _mix_out_ln_kernel(a_ref, b_ref, wa_ref, wb_ref, res_ref, g_ref, bb_ref, out_ref):
    m = (jnp.dot(a_ref[...].astype(BF16), wa_ref[...], preferred_element_type=F32)
         + jnp.dot(b_ref[...].astype(BF16), wb_ref[...], preferred_element_type=F32))
    out_ref[...] = _layer_norm_rows(DN_ALPHA * res_ref[...] + m, g_ref[...], bb_ref[...])


def _mix_out_ln(y_hy, y_mla, w_out, res, ln_g, ln_b, *, tm=512):
    T, D = res.shape
    tm = min(tm, T)
    wa = w_out[:HY_W].astype(BF16)
    wb = w_out[HY_W:].astype(BF16)
    row = lambda w: pl.BlockSpec((tm, w), lambda i: (i, 0))
    full = lambda a: pl.BlockSpec(a.shape, lambda i: (0, 0))
    vec = pl.BlockSpec((1, D), lambda i: (0, 0))
    return pl.pallas_call(
        _mix_out_ln_kernel,
        grid=(T // tm,),
        in_specs=[row(HY_W), row(MLA_V_W), full(wa), full(wb), row(D), vec, vec],
        out_specs=row(D),
        out_shape=jax.ShapeDtypeStruct((T, D), F32),
        compiler_params=_cparams(("parallel",)),
        name="mix_out_ln",
    )(y_hy, y_mla, wa, wb, res, ln_g.reshape(1, D), ln_b.reshape(1, D))


def _hyena_mla_layer(x, p, kspec, tabs, ln_g, ln_b):
    B, L, D = x.shape
    u, q, k, v = _in_proj(x, p['w_in'], p['w_uq'], p['w_ukv'], p['q_norm'], p['kv_norm'])
    y_hy = _hyena(u, p['conv_w'], p['conv_b'], p['skip'], kspec, tabs)
    y_mla = _attention(q, k, v)
    return _mix_out_ln(y_hy.reshape(B * L, HY_W), y_mla.reshape(B * L, MLA_V_W), p['w_out'],
                       x.reshape(B * L, D), ln_g, ln_b).reshape(B, L, D)


MOE_ROWS = 256
ROUTE_W = LANES
NEG_BIG = -1e30


def _router_kernel(x_ref, w_ref, idx_out, gate_out):
    logits = _hdot(x_ref[...], w_ref[...])
    lane = lax.broadcasted_iota(jnp.int32, logits.shape, 1)
    is_group = lane < MOE_GROUPS
    gl = jnp.where(is_group, logits, NEG_BIG)
    gmax = jnp.max(gl, axis=-1, keepdims=True)
    g_top = jnp.min(jnp.where(gl == gmax, lane, ROUTE_W), axis=-1, keepdims=True)
    g_prob = 1.0 / jnp.sum(jnp.where(is_group, jnp.exp(gl - gmax), 0.0), axis=-1, keepdims=True)

    first = MOE_GROUPS + g_top * MOE_PER_GROUP
    in_grp = jnp.logical_and(lane >= first, lane < first + MOE_PER_GROUP)
    el = jnp.where(in_grp, logits, NEG_BIG)
    emax = jnp.max(el, axis=-1, keepdims=True)
    ex = jnp.where(in_grp, jnp.exp(el - emax), 0.0)
    prob = ex / jnp.sum(ex, axis=-1, keepdims=True)
    pm = jnp.where(in_grp, prob, -1.0)
    p1 = jnp.max(pm, axis=-1, keepdims=True)
    l1 = jnp.min(jnp.where(pm == p1, lane, ROUTE_W), axis=-1, keepdims=True)
    pm2 = jnp.where(lane == l1, -1.0, pm)
    p2 = jnp.max(pm2, axis=-1, keepdims=True)
    l2 = jnp.min(jnp.where(pm2 == p2, lane, ROUTE_W), axis=-1, keepdims=True)
    tot = p1 + p2
    e1 = l1 - MOE_GROUPS
    e2 = l2 - MOE_GROUPS
    idx_out[...] = jnp.where(lane == 0, e1, jnp.where(lane == 1, e2, 0))
    gate_out[...] = jnp.where(lane == 0, g_prob * (p1 / tot), jnp.where(lane == 1, g_prob * (p2 / tot), 0.0))


def _router(xt, w_group, w_expert, *, tm=512):
    T, D = xt.shape
    tm = min(tm, T)
    w = _pad_axis(jnp.concatenate([w_group, w_expert], axis=1), 1, ROUTE_W)
    row = pl.BlockSpec((tm, ROUTE_W), lambda i: (i, 0))
    return pl.pallas_call(
        _router_kernel,
        grid=(T // tm,),
        in_specs=[pl.BlockSpec((tm, D), lambda i: (i, 0)), pl.BlockSpec((D, ROUTE_W), lambda i: (0, 0))],
        out_specs=[row, row],
        out_shape=[jax.ShapeDtypeStruct((T, ROUTE_W), jnp.int32), jax.ShapeDtypeStruct((T, ROUTE_W), F32)],
        compiler_params=_cparams(("parallel",)),
        name="moe_router",
    )(xt, w)


def _gather_rows(src_hbm, idx_ref, idx_base, buf, sem, n_rows):
    def row_copy(r):
        return pltpu.make_async_copy(src_hbm.at[pl.ds(idx_ref[0, idx_base + r], 1)], buf.at[pl.ds(r, 1)], sem)

    def start(r, c):
        row_copy(r).start()
        return c

    def wait(r, c):
        row_copy(r).wait()
        return c

    lax.fori_loop(0, n_rows, start, 0, unroll=8)
    lax.fori_loop(0, n_rows, wait, 0, unroll=8)


def _expert_kernel(blk_e_ref, n_used_ref, tok_ref, x_hbm, w1_ref, w3_ref, w2_ref, o_ref, xbuf, sem):
    i = pl.program_id(0)

    @pl.when(i < n_used_ref[0])
    def _():
        _gather_rows(x_hbm, tok_ref, 0, xbuf, sem, MOE_ROWS)
        xb = xbuf[...].astype(BF16)
        h1 = jnp.dot(xb, w1_ref[...], preferred_element_type=F32)
        h3 = jnp.dot(xb, w3_ref[...], preferred_element_type=F32)
        h = (h1 * jax.nn.sigmoid(h1) * h3).astype(BF16)
        o_ref[...] = jnp.dot(h, w2_ref[...], preferred_element_type=F32)

    @pl.when(i >= n_used_ref[0])
    def _():
        o_ref[...] = jnp.zeros_like(o_ref)


def _experts(xt, src_tok, blk_e, n_used, w1, w3, w2):
    T, D = xt.shape
    n_blocks = blk_e.shape[0]
    wspec = lambda shape: pl.BlockSpec((None,) + shape, lambda i, be, nu: (be[i], 0, 0))
    return pl.pallas_call(
        _expert_kernel,
        grid_spec=pltpu.PrefetchScalarGridSpec(
            num_scalar_prefetch=2,
            grid=(n_blocks,),
            in_specs=[pl.BlockSpec((None, 1, MOE_ROWS), lambda i, be, nu: (i, 0, 0), memory_space=pltpu.SMEM),
                      pl.BlockSpec(memory_space=pl.ANY),
                      wspec((D, MOE_FF)), wspec((D, MOE_FF)), wspec((MOE_FF, D))],
            out_specs=pl.BlockSpec((MOE_ROWS, D), lambda i, be, nu: (i, 0)),
            scratch_shapes=[pltpu.VMEM((MOE_ROWS, D), F32), pltpu.SemaphoreType.DMA(())],
        ),
        out_shape=jax.ShapeDtypeStruct((n_blocks * MOE_ROWS, D), F32),
        compiler_params=_cparams(("arbitrary",)),
        name="moe_experts",
    )(blk_e, n_used, src_tok.reshape(n_blocks, 1, MOE_ROWS), xt, w1, w3, w2)


def _combine_ln_kernel(dest_ref, eo_hbm, gate_ref, x_ref, g_ref, b_ref, out_ref, buf0, buf1, sems):
    tm = x_ref.shape[0]
    _gather_rows(eo_hbm, dest_ref, 0, buf0, sems.at[0], tm)
    _gather_rows(eo_hbm, dest_ref, tm, buf1, sems.at[1], tm)
    gate = gate_ref[...]
    f = gate[:, 0:1] * buf0[...] + gate[:, 1:2] * buf1[...]
    out_ref[...] = _layer_norm_rows(DN_ALPHA * x_ref[...] + f, g_ref[...], b_ref[...])


def _combine_ln(eo, dest, gate, xt, ln_g, ln_b, *, tm=256):
    T, D = xt.shape
    tm = min(tm, T)
    nt = T // tm
    dest_t = dest.reshape(nt, tm, MOE_TOPK).transpose(0, 2, 1).reshape(nt, 1, MOE_TOPK * tm)
    row = pl.BlockSpec((tm, D), lambda i: (i, 0))
    vec = pl.BlockSpec((1, D), lambda i: (0, 0))
    return pl.pallas_call(
        _combine_ln_kernel,
        grid=(nt,),
        in_specs=[pl.BlockSpec((None, 1, MOE_TOPK * tm), lambda i: (i, 0, 0), memory_space=pltpu.SMEM),
                  pl.BlockSpec(memory_space=pl.ANY), pl.BlockSpec((tm, ROUTE_W), lambda i: (i, 0)), row, vec, vec],
        out_specs=row,
        out_shape=jax.ShapeDtypeStruct((T, D), F32),
        scratch_shapes=[pltpu.VMEM((tm, D), F32), pltpu.VMEM((tm, D), F32), pltpu.SemaphoreType.DMA((2,))],
        compiler_params=_cparams(("arbitrary",)),
        name="moe_combine_ln",
    )(dest_t, eo, gate, xt, ln_g.reshape(1, D), ln_b.reshape(1, D))


def _dispatch_plan(expert):
    T = expert.shape[0]
    A = T * MOE_TOPK
    n_blocks = A // MOE_ROWS + MOE_EXPERTS
    flat_e = expert.reshape(-1)
    order = jnp.argsort(flat_e, stable=True)
    e_sorted = flat_e[order]
    counts = jnp.bincount(flat_e, length=MOE_EXPERTS)
    padded = (counts + MOE_ROWS - 1) // MOE_ROWS * MOE_ROWS
    pad_end = jnp.cumsum(padded)
    pad_start = pad_end - padded
    start = jnp.cumsum(counts) - counts
    dest_sorted = (pad_start[e_sorted] + jnp.arange(A) - start[e_sorted]).astype(jnp.int32)
    dest = jnp.zeros((A,), jnp.int32).at[order].set(dest_sorted)
    src_tok = jnp.zeros((n_blocks * MOE_ROWS,), jnp.int32).at[dest_sorted].set((order // MOE_TOPK).astype(jnp.int32))
    blk_e = jnp.minimum(jnp.searchsorted(pad_end, jnp.arange(n_blocks) * MOE_ROWS, side='right'),
                        MOE_EXPERTS - 1).astype(jnp.int32)
    n_used = (pad_end[-1] // MOE_ROWS).astype(jnp.int32).reshape(1)
    return dest.reshape(T, MOE_TOPK), src_tok, blk_e, n_used


def _moe_layer(x, w_group, w_expert, w1, w3, w2, ln_g, ln_b):
    B, L, D = x.shape
    xt = x.reshape(B * L, D)
    idx, gate = _router(xt, w_group, w_expert)
    dest, src_tok, blk_e, n_used = _dispatch_plan(idx[:, :MOE_TOPK])
    eo = _experts(xt, src_tok, blk_e, n_used, w1, w3, w2)
    return _combine_ln(eo, dest, gate, xt, ln_g, ln_b).reshape(B, L, D)


def _trunk(x, p, tabs, kspecs):
    for layer in range(DEPTH):
        i = layer // 2
        if layer % 2 == 0:
            lp = {n: p[n][i] for n in ('w_in', 'w_uq', 'w_ukv', 'q_norm', 'kv_norm', 'conv_w', 'conv_b', 'skip', 'w_out')}
            x = _hyena_mla_layer(x, lp, kspecs[i], tabs, p['ln1_g'][layer], p['ln1_b'][layer])
        else:
            lp = {n[3:]: p[n][i] for n in p if n.startswith('rw_')}
            x = _rwkv_layer(x, lp, p['ln1_g'][layer], p['ln1_b'][layer])
        x = _moe_layer(x, p['moe_w_group'][layer], p['moe_w_expert'][layer], p['moe_w1'][layer],
                       p['moe_w3'][layer], p['moe_w2'][layer], p['ln2_g'][layer], p['ln2_b'][layer])
    return x


def kernel(x_prompt, x_sample, ln1_g, ln1_b, ln2_g, ln2_b, mix_w_in, hy_conv_w, hy_conv_b, hy_ffn_w1,
           hy_ffn_b1, hy_ffn_w2, hy_ffn_b2, hy_ffn_w3, hy_sin_freq, hy_decay, hy_skip, mla_q_norm, mla_w_uq,
           mla_kv_norm, mla_w_ukv, mix_w_out, rw_mu, rw_w_r, rw_w_k, rw_w_v, rw_w0, rw_w1, rw_w2, rw_a0,
           rw_a1, rw_a2, rw_g1, rw_g2, rw_k_k, rw_k_a, rw_r_k, rw_gn_g, rw_gn_b, rw_w_o, moe_w_group,
           moe_w_expert, moe_w1, moe_w3, moe_w2):
    p = dict(ln1_g=ln1_g, ln1_b=ln1_b, ln2_g=ln2_g, ln2_b=ln2_b, w_in=mix_w_in, conv_w=hy_conv_w,
             conv_b=hy_conv_b, skip=hy_skip, q_norm=mla_q_norm, w_uq=mla_w_uq, kv_norm=mla_kv_norm,
             w_ukv=mla_w_ukv, w_out=mix_w_out, rw_mu=rw_mu, rw_w_r=rw_w_r, rw_w_k=rw_w_k, rw_w_v=rw_w_v,
             rw_w0=rw_w0, rw_w1=rw_w1, rw_w2=rw_w2, rw_a0=rw_a0, rw_a1=rw_a1, rw_a2=rw_a2, rw_g1=rw_g1,
             rw_g2=rw_g2, rw_k_k=rw_k_k, rw_k_a=rw_k_a, rw_r_k=rw_r_k, rw_gn_g=rw_gn_g, rw_gn_b=rw_gn_b,
             rw_w_o=rw_w_o, moe_w_group=moe_w_group, moe_w_expert=moe_w_expert,
             moe_w1=moe_w1.astype(BF16), moe_w3=moe_w3.astype(BF16), moe_w2=moe_w2.astype(BF16))
    outs = []
    for x in (x_prompt, x_sample):
        L = x.shape[1]
        tabs = _dft_tables(L)
        kspecs = [_hyena_filter_spectrum(L, hy_ffn_w1[i], hy_ffn_b1[i], hy_ffn_w2[i], hy_ffn_b2[i],
                                         hy_ffn_w3[i], hy_sin_freq[i], hy_decay[i], tabs)
                  for i in range((DEPTH + 1) // 2)]
        outs.append(_trunk(x, p, tabs, kspecs))
    return tuple(outs)
```

```python
import functools
import math

import jax
import jax.numpy as jnp
import numpy as np
from jax import lax
from jax.experimental import pallas as pl
from jax.experimental.pallas import tpu as pltpu

F32 = jnp.float32
BF16 = jnp.bfloat16

LANES = 128
VMEM_LIMIT_BYTES = 56 * 1024 * 1024

D_MODEL = 1024
DEPTH = 4
HY_W = D_MODEL // 2
HY_ORDER = 2
HY_BANDS = 16
MLA_HEADS = 8
MLA_NOPE = 64
MLA_ROPE = 32
MLA_V = 64
MLA_Q_LORA = 256
MLA_KV_LORA = 128
ROPE_THETA = 10000.0
RW_HEAD = 64
RW_HEADS = D_MODEL // RW_HEAD
RW_GN_EPS = 64e-5
MOE_GROUPS = 4
MOE_PER_GROUP = 8
MOE_EXPERTS = MOE_GROUPS * MOE_PER_GROUP
MOE_TOPK = 2
MOE_FF = 512
LN_EPS = 1e-5
RMS_EPS = 1e-6
DN_ALPHA = (2 * DEPTH) ** 0.25

RW_CHUNK = 64
HEADS_PER_SLAB = LANES // RW_HEAD


def _cparams(sem):
    return pltpu.CompilerParams(dimension_semantics=sem, vmem_limit_bytes=VMEM_LIMIT_BYTES)


def _run_staged(chains, skew):
    done = [False] * len(chains)
    rnd = 0
    while not all(done):
        for c, ch in enumerate(chains):
            if rnd >= skew * c and not done[c]:
                done[c] = not next(ch, False)
        rnd += 1


def _bdot(a, b):
    return jnp.dot(a.astype(BF16), b.astype(BF16), preferred_element_type=F32)


def _bdot_nt(a, b):
    return lax.dot_general(a.astype(BF16), b.astype(BF16), (((1,), (1,)), ((), ())),
                           preferred_element_type=F32)


def _bdot_tn(a, b):
    return lax.dot_general(a.astype(BF16), b.astype(BF16), (((0,), (0,)), ((), ())),
                           preferred_element_type=F32)


def _rwkv_scan_kernel(r_ref, lw_ref, k_ref, v_ref, kk_ref, b_ref, g_ref, rk_ref, gng_ref, gnb_ref,
                      o_ref, s_ref, *, tb):
    C = RW_CHUNK
    z = pl.program_id(0)
    fwd = z == 0

    @pl.when(pl.program_id(3) == 0)
    def _():
        s_ref[...] = jnp.zeros_like(s_ref)

    lane = lax.broadcasted_iota(jnp.int32, (1, LANES), 1)
    m0 = lane < RW_HEAD
    sgn = jnp.where(fwd, 1, -1)
    ri = lax.broadcasted_iota(jnp.int32, (C, C), 0)
    ci = lax.broadcasted_iota(jnp.int32, (C, C), 1)
    tri_c = jnp.where((ci - ri) * sgn <= 0, 1.0, 0.0).astype(BF16)
    r2 = lax.broadcasted_iota(jnp.int32, (2 * C, 2 * C), 0)
    c2 = lax.broadcasted_iota(jnp.int32, (2 * C, 2 * C), 1)
    head_r = jnp.where(r2 >= C, 1, 0)
    head_c = jnp.where(c2 >= C, 1, 0)
    lag = jnp.where(head_r == head_c, (c2 - r2) * sgn, 1)
    strict = lag < 0
    incl = lag <= 0

    def stack(x):
        return jnp.concatenate([jnp.where(m0, x, 0.0), jnp.where(m0, 0.0, x)], axis=0)

    def head_sum(x):
        s0 = jnp.sum(jnp.where(m0, x, 0.0), axis=-1, keepdims=True)
        s1 = jnp.sum(jnp.where(m0, 0.0, x), axis=-1, keepdims=True)
        return jnp.where(m0, s0, s1)

    def chunk(j, carry):
        start = jnp.where(fwd, j * C, tb - (j + 1) * C)
        start = pl.multiple_of(start, C)
        sl = pl.ds(start, C)
        _run_staged([slab_chunk(sl, slab) for slab in range(s_ref.shape[0])], skew=0)
        return carry

    def slab_chunk(sl, slab):
        ln = slice(slab * LANES, (slab + 1) * LANES)
        rk = rk_ref[:, ln]
        gng = gng_ref[:, ln]
        gnb = gnb_ref[:, ln]
        r = r_ref[sl, ln]
        lw = lw_ref[sl, ln]
        k = k_ref[sl, ln]
        v = v_ref[sl, ln]
        kk = kk_ref[sl, ln]
        b = b_ref[sl, ln]
        g = g_ref[sl, ln]
        S = s_ref[slab]

        lw_hi = lw.astype(BF16)
        rem = lw - lw_hi.astype(F32)
        lw_mid = rem.astype(BF16)
        lw_lo = (rem - lw_mid.astype(F32)).astype(BF16)
        cl3 = jnp.dot(tri_c, jnp.concatenate([lw_hi, lw_mid, lw_lo], axis=1), preferred_element_type=F32)
        cl = cl3[:, :LANES] + cl3[:, LANES:2 * LANES] + cl3[:, 2 * LANES:]
        yield True
        cl_last = jnp.where(fwd, cl[C - 1:C, :], cl[0:1, :])
        eneg = jnp.exp(-cl)
        epos = jnp.exp(cl)
        ehat = jnp.exp(cl_last - cl)
        at = -kk * jnp.exp(cl - lw)
        la_lr = jnp.concatenate([stack(at), stack(r * epos)], axis=0).astype(BF16)
        rb_rk = jnp.concatenate([stack(b * eneg), stack(k * eneg)], axis=0).astype(BF16)
        v2 = stack(v)

        sc = _bdot_nt(la_lr, rb_rk)
        hs = _bdot_nt(la_lr, S)
        ys = hs[2 * C:]
        yield True
        mab = jnp.where(strict, sc[:2 * C, :2 * C], 0.0)
        mak = jnp.where(strict, sc[:2 * C, 2 * C:], 0.0)
        nrb = jnp.where(incl, sc[2 * C:, :2 * C], 0.0)
        nrk = jnp.where(incl, sc[2 * C:, 2 * C:], 0.0)
        x = hs[:2 * C] + _bdot(mak, v2)
        mp = mab
        yield True
        x = x + _bdot(mp, x)
        for _ in range(int(math.log2(C)) - 1):
            mp = _bdot(mp, mp)
            yield True
            x = x + _bdot(mp, x)
        yield True
        uv = jnp.concatenate([x, v2], axis=0)
        y2 = ys + _bdot(jnp.concatenate([nrb, nrk], axis=1), uv)
        bh_kh = jnp.concatenate([stack(b * ehat), stack(k * ehat)], axis=0)
        s_ref[slab] = S * jnp.exp(cl_last) + _bdot_tn(uv, bh_kh)
        yield True
        y = y2[:C] + y2[C:]

        mean = head_sum(y) * (1.0 / RW_HEAD)
        d = y - mean
        var = head_sum(d * d) * (1.0 / RW_HEAD)
        yn = d * lax.rsqrt(var + RW_GN_EPS) * gng + gnb
        bonus = head_sum(r * k * rk) * v
        o_ref[sl, ln] = (yn + bonus) * g

    lax.fori_loop(0, tb // C, chunk, 0)


RW_SLABS_PER_STEP = 8


def _rwkv_scan(r, lw, k, v, kk, b, g, r_k, gn_g, gn_b, *, tb=512):
    Z, B, L, D = r.shape
    tb = min(tb, L)
    nblk = L // tb
    ns = min(RW_SLABS_PER_STEP, D // LANES)
    width = ns * LANES

    def amap(z, bb, h, i):
        return (z, bb, jnp.where(z == 0, i, nblk - 1 - i), h)

    act = pl.BlockSpec((None, None, tb, width), amap)
    par = pl.BlockSpec((1, width), lambda z, bb, h, i: (0, h))
    return pl.pallas_call(
        functools.partial(_rwkv_scan_kernel, tb=tb),
        grid=(Z, B, D // width, nblk),
        in_specs=[act] * 7 + [par] * 3,
        out_specs=act,
        out_shape=jax.ShapeDtypeStruct((Z, B, L, D), F32),
        scratch_shapes=[pltpu.VMEM((ns, LANES, LANES), F32)],
        compiler_params=_cparams(("parallel", "parallel", "parallel", "arbitrary")),
        name="rwkv_scan",
    )(r, lw, k, v, kk, b, g, r_k.reshape(1, D), gn_g.reshape(1, D), gn_b.reshape(1, D))


HALO_ROWS = 8


def _halo_specs(tm, width, seq_len, lead):
    per = tm // HALO_ROWS
    last = seq_len // HALO_ROWS - 1

    def main(*g):
        return (g[lead], g[lead + 1], 0)

    def prev(*g):
        return (g[lead], jnp.maximum(g[lead + 1] * per - 1, 0), 0)

    def nxt(*g):
        return (g[lead], jnp.minimum((g[lead + 1] + 1) * per, last), 0)

    return (pl.BlockSpec((None, tm, width), main), pl.BlockSpec((None, HALO_ROWS, width), prev),
            pl.BlockSpec((None, HALO_ROWS, width), nxt))


def _shifted(x, prev_ref, next_ref, i, n_tiles):
    tm = x.shape[0]
    prev_row = jnp.where(i == 0, 0.0, prev_ref[HALO_ROWS - 1:HALO_ROWS, :])
    next_row = jnp.where(i == n_tiles - 1, 0.0, next_ref[0:1, :])
    row = lax.broadcasted_iota(jnp.int32, (tm, 1), 0)
    x_prev = jnp.where(row == 0, prev_row, pltpu.roll(x, 1, axis=0))
    x_next = jnp.where(row == tm - 1, next_row, pltpu.roll(x, tm - 1, axis=0))
    return x_prev, x_next


def _layer_norm_rows(y, g, b):
    mu = jnp.mean(y, axis=-1, keepdims=True)
    d = y - mu
    var = jnp.mean(d * d, axis=-1, keepdims=True)
    return d * lax.rsqrt(var + LN_EPS) * g + b


RW_LORA_PAD = 128
RW_GATE_PAD = 256
RW_DECAY_SCALE = math.exp(-0.5)


def _rwkv_proj_kernel(x_ref, xp_ref, xn_ref, mu_ref, wr_ref, wk_ref, wv_ref, w0_ref, w1_ref, w2_ref,
                      a0_ref, a1_ref, a2_ref, g1_ref, g2_ref, kk_ref, ka_ref,
                      r_out, lw_out, k_out, v_out, kk_out, b_out, g_out):
    z = pl.program_id(0)
    i = pl.program_id(2)
    x = x_ref[...]
    x_prev, x_next = _shifted(x, xp_ref, xn_ref, i, pl.num_programs(2))
    xx = jnp.where(z == 0, x_prev, x_next) - x
    mu = mu_ref[...]

    def mix(j):
        return (x + xx * mu[j:j + 1, :]).astype(BF16)

    r = jnp.dot(mix(0), wr_ref[...], preferred_element_type=F32)
    k = jnp.dot(mix(2), wk_ref[...], preferred_element_type=F32)
    v = jnp.dot(mix(3), wv_ref[...], preferred_element_type=F32)
    lw = w0_ref[...] + _bdot(jnp.tanh(jnp.dot(mix(1), w1_ref[...], preferred_element_type=F32)), w2_ref[...])
    logw = -jax.nn.sigmoid(lw) * RW_DECAY_SCALE
    a = jax.nn.sigmoid(a0_ref[...] + _bdot(jnp.dot(mix(4), a1_ref[...], preferred_element_type=F32), a2_ref[...]))
    g = _bdot(jax.nn.sigmoid(jnp.dot(mix(5), g1_ref[...], preferred_element_type=F32)), g2_ref[...])

    kkr = k * kk_ref[...]
    lane = lax.broadcasted_iota(jnp.int32, (1, LANES), 1)
    m0 = lane < RW_HEAD
    slabs = []
    for s in range(x.shape[1] // LANES):
        q = kkr[:, s * LANES:(s + 1) * LANES]
        q2 = q * q
        n0 = jnp.sum(jnp.where(m0, q2, 0.0), axis=-1, keepdims=True)
        n1 = jnp.sum(jnp.where(m0, 0.0, q2), axis=-1, keepdims=True)
        nrm = jnp.maximum(jnp.sqrt(jnp.where(m0, n0, n1)), 1e-12)
        slabs.append(q / nrm)
    kk = jnp.concatenate(slabs, axis=1)

    r_out[...] = r
    lw_out[...] = logw
    k_out[...] = k * (1.0 + (a - 1.0) * ka_ref[...])
    v_out[...] = v
    kk_out[...] = kk
    b_out[...] = kk * a
    g_out[...] = g


def _pad_axis(w, axis, size):
    pad = [(0, 0)] * w.ndim
    pad[axis] = (0, size - w.shape[axis])
    return jnp.pad(w, pad)


def _rwkv_proj(x, p, *, tm=256):
    B, L, D = x.shape
    tm = min(tm, L)
    main, prev, nxt = _halo_specs(tm, D, L, lead=1)
    full2 = lambda shape: pl.BlockSpec(shape, lambda z, b, i: (0, 0))
    perz = lambda shape: pl.BlockSpec((None,) + shape, lambda z, b, i: (z, 0, 0))
    out_spec = pl.BlockSpec((None, None, tm, D), lambda z, b, i: (z, b, i, 0))
    w1 = _pad_axis(p['w1'], 2, RW_LORA_PAD).astype(BF16)
    w2 = _pad_axis(p['w2'], 1, RW_LORA_PAD).astype(BF16)
    a1 = _pad_axis(p['a1'], 2, RW_LORA_PAD).astype(BF16)
    a2 = _pad_axis(p['a2'], 1, RW_LORA_PAD).astype(BF16)
    g1 = _pad_axis(p['g1'], 2, RW_GATE_PAD).astype(BF16)
    g2 = _pad_axis(p['g2'], 1, RW_GATE_PAD).astype(BF16)
    outs = pl.pallas_call(
        _rwkv_proj_kernel,
        grid=(2, B, L // tm),
        in_specs=[main, prev, nxt, perz((6, D)), full2((D, D)), full2((D, D)), full2((D, D)),
                  perz((1, D)), perz((D, RW_LORA_PAD)), perz((RW_LORA_PAD, D)),
                  perz((1, D)), perz((D, RW_LORA_PAD)), perz((RW_LORA_PAD, D)),
                  perz((D, RW_GATE_PAD)), perz((RW_GATE_PAD, D)), full2((1, D)), full2((1, D))],
        out_specs=[out_spec] * 7,
        out_shape=[jax.ShapeDtypeStruct((2, B, L, D), F32)] * 7,
        compiler_params=_cparams(("parallel", "parallel", "parallel")),
        name="rwkv_proj",
    )(x, x, x, p['mu'], p['w_r'].astype(BF16), p['w_k'].astype(BF16), p['w_v'].astype(BF16),
      p['w0'].reshape(2, 1, D), w1, w2, p['a0'].reshape(2, 1, D), a1, a2, g1, g2,
      p['k_k'].reshape(1, D), p['k_a'].reshape(1, D))
    return outs


def _rw_out_ln_kernel(o_ref, w_ref, res_ref, g_ref, b_ref, out_ref):
    s = (o_ref[0] + o_ref[1]).astype(BF16)
    m = jnp.dot(s, w_ref[...], preferred_element_type=F32)
    out_ref[...] = _layer_norm_rows(DN_ALPHA * res_ref[...] + m, g_ref[...], b_ref[...])


def _rw_out_ln(o, w_o, res, ln_g, ln_b, *, tm=512):
    _, T, D = o.shape
    tm = min(tm, T)
    row = pl.BlockSpec((tm, D), lambda i: (i, 0))
    vec = pl.BlockSpec((1, D), lambda i: (0, 0))
    return pl.pallas_call(
        _rw_out_ln_kernel,
        grid=(T // tm,),
        in_specs=[pl.BlockSpec((2, tm, D), lambda i: (0, i, 0)), pl.BlockSpec((D, D), lambda i: (0, 0)),
                  row, vec, vec],
        out_specs=row,
        out_shape=jax.ShapeDtypeStruct((T, D), F32),
        compiler_params=_cparams(("parallel",)),
        name="rw_out_ln",
    )(o, w_o.astype(BF16), res, ln_g.reshape(1, D), ln_b.reshape(1, D))


def _rwkv_layer(x, p, ln_g, ln_b):
    B, L, D = x.shape
    r, lw, k, v, kk, b, g = _rwkv_proj(x, p)
    o = _rwkv_scan(r, lw, k, v, kk, b, g, p['r_k'].reshape(D), p['gn_g'], p['gn_b'])
    return _rw_out_ln(o.reshape(2, B * L, D), p['w_o'], x.reshape(B * L, D), ln_g, ln_b).reshape(B, L, D)


HY_U_W = (HY_ORDER + 1) * HY_W
MLA_HEAD_PAD = LANES
MLA_QK_W = MLA_HEADS * MLA_HEAD_PAD
MLA_V_W = MLA_HEADS * MLA_V
MLA_SCALE = (MLA_NOPE + MLA_ROPE) ** -0.5 * math.log2(math.e)
HALF_ROPE = MLA_ROPE // 2


def _rms_rows(c, g):
    return c * lax.rsqrt(jnp.mean(c * c, axis=-1, keepdims=True) + RMS_EPS) * g


def _in_proj_kernel(x_ref, wu_ref, ws_ref, wq_ref, wkv_ref, qn_ref, kvn_ref, cos_ref, sin_ref,
                    u_out, q_out, k_out, v_out):
    xb = x_ref[...].astype(BF16)
    u_out[...] = jnp.dot(xb, wu_ref[...], preferred_element_type=F32)
    small = jnp.dot(xb, ws_ref[...], preferred_element_type=F32)
    c_q = small[:, :MLA_Q_LORA]
    c_kv = small[:, MLA_Q_LORA:MLA_Q_LORA + MLA_KV_LORA]
    kpe_a = small[:, MLA_Q_LORA + MLA_KV_LORA:MLA_Q_LORA + MLA_KV_LORA + LANES]
    kpe_b = small[:, MLA_Q_LORA + MLA_KV_LORA + LANES:]
    cos = cos_ref[...]
    sin = sin_ref[...]
    cos_h = jnp.concatenate([cos] * MLA_HEADS, axis=1)
    sin_h = jnp.concatenate([sin] * MLA_HEADS, axis=1)

    qq = _bdot(_rms_rows(c_q, qn_ref[...]), wq_ref[...])
    q_out[...] = (qq[:, :MLA_QK_W] * cos_h + qq[:, MLA_QK_W:] * sin_h).astype(BF16)

    kv = _bdot(_rms_rows(c_kv, kvn_ref[...]), wkv_ref[...])
    kpe = kpe_a * cos + kpe_b * sin
    k_out[...] = (kv[:, :MLA_QK_W] + jnp.concatenate([kpe] * MLA_HEADS, axis=1)).astype(BF16)
    v_out[...] = kv[:, MLA_QK_W:].T.astype(BF16)


def _rope_head_tables(L):
    inv = ROPE_THETA ** (-jnp.arange(0, MLA_ROPE, 2, dtype=F32) / MLA_ROPE)
    ang = jnp.arange(L, dtype=F32)[:, None] * inv[None, :]
    cos, sin = jnp.cos(ang), jnp.sin(ang)
    ones = jnp.ones((L, MLA_NOPE), F32)
    zn = jnp.zeros((L, MLA_NOPE), F32)
    zp = jnp.zeros((L, MLA_HEAD_PAD - MLA_NOPE - MLA_ROPE), F32)
    return (jnp.concatenate([ones, cos, cos, zp], axis=1), jnp.concatenate([zn, sin, sin, zp], axis=1))


def _mla_weights(w_in, w_uq, w_ukv):
    s1 = HY_U_W
    s2 = s1 + MLA_Q_LORA
    s3 = s2 + MLA_KV_LORA
    kx1, kx2 = w_in[:, s3:s3 + HALF_ROPE], w_in[:, s3 + HALF_ROPE:]
    zl = jnp.zeros((w_in.shape[0], MLA_NOPE), F32)
    zr = jnp.zeros((w_in.shape[0], MLA_HEAD_PAD - MLA_NOPE - MLA_ROPE), F32)
    kpe_a = jnp.concatenate([zl, kx1, kx2, zr], axis=1)
    kpe_b = jnp.concatenate([zl, -kx2, kx1, zr], axis=1)
    w_small = jnp.concatenate([w_in[:, s1:s3], kpe_a, kpe_b], axis=1)
    q3 = w_uq.reshape(MLA_Q_LORA, MLA_HEADS, MLA_NOPE + MLA_ROPE) * MLA_SCALE
    nope, x1, x2 = q3[..., :MLA_NOPE], q3[..., MLA_NOPE:MLA_NOPE + HALF_ROPE], q3[..., MLA_NOPE + HALF_ROPE:]
    zq = jnp.zeros((MLA_Q_LORA, MLA_HEADS, MLA_HEAD_PAD - MLA_NOPE - MLA_ROPE), F32)
    wqa = jnp.concatenate([nope, x1, x2, zq], axis=-1).reshape(MLA_Q_LORA, MLA_QK_W)
    wqb = jnp.concatenate([jnp.zeros_like(nope), -x2, x1, zq], axis=-1).reshape(MLA_Q_LORA, MLA_QK_W)
    kv3 = w_ukv.reshape(MLA_KV_LORA, MLA_HEADS, MLA_NOPE + MLA_V)
    wkn = jnp.concatenate([kv3[..., :MLA_NOPE], jnp.zeros((MLA_KV_LORA, MLA_HEADS, MLA_HEAD_PAD - MLA_NOPE), F32)],
                          axis=-1).reshape(MLA_KV_LORA, MLA_QK_W)
    wv = kv3[..., MLA_NOPE:].reshape(MLA_KV_LORA, MLA_V_W)
    return (w_in[:, :s1].astype(BF16), w_small.astype(BF16), jnp.concatenate([wqa, wqb], axis=1).astype(BF16),
            jnp.concatenate([wkn, wv], axis=1).astype(BF16))


def _in_proj(x, w_in, w_uq, w_ukv, q_norm, kv_norm, *, tm=256):
    B, L, D = x.shape
    tm = min(tm, L)
    wu, ws, wq, wkv = _mla_weights(w_in, w_uq, w_ukv)
    cos, sin = _rope_head_tables(L)
    row = lambda w: pl.BlockSpec((None, tm, w), lambda b, i: (b, i, 0))
    full = lambda a: pl.BlockSpec(a.shape, lambda b, i: (0, 0))
    pos = pl.BlockSpec((tm, LANES), lambda b, i: (i, 0))
    qn = q_norm.reshape(1, -1)
    kvn = kv_norm.reshape(1, -1)
    return pl.pallas_call(
        _in_proj_kernel,
        grid=(B, L // tm),
        in_specs=[row(D), full(wu), full(ws), full(wq), full(wkv), full(qn), full(kvn), pos, pos],
        out_specs=[row(HY_U_W), row(MLA_QK_W), row(MLA_QK_W),
                   pl.BlockSpec((None, MLA_V_W, tm), lambda b, i: (b, 0, i))],
        out_shape=[jax.ShapeDtypeStruct((B, L, HY_U_W), F32), jax.ShapeDtypeStruct((B, L, MLA_QK_W), BF16),
                   jax.ShapeDtypeStruct((B, L, MLA_QK_W), BF16), jax.ShapeDtypeStruct((B, MLA_V_W, L), BF16)],
        compiler_params=_cparams(("parallel", "parallel")),
        name="in_proj",
    )(x, wu, ws, wq, wkv, qn, kvn, cos, sin)


MLA_PAIR = 2
ATTN_SUM_ROWS = 8


def _attn_kernel(q_ref, k_ref, vt_ref, o_ref, s_ref, p_ref, mx_ref, al_ref, m_ref, l_ref, acc_ref, *, tk):
    tq = q_ref.shape[0]
    nk = k_ref.shape[0] // tk
    heads = range(MLA_PAIR)
    head_lanes = lambda h: slice(h * MLA_HEAD_PAD, (h + 1) * MLA_HEAD_PAD)
    head_rows = lambda h: slice(h * MLA_V, (h + 1) * MLA_V)

    m_ref[...] = jnp.full_like(m_ref, -jnp.inf)
    l_ref[...] = jnp.zeros_like(l_ref)
    acc_ref[...] = jnp.zeros_like(acc_ref)
    p_ref[1] = jnp.zeros_like(p_ref[1])
    al_ref[1] = jnp.ones_like(al_ref[1])

    ones = jnp.ones((ATTN_SUM_ROWS, tk), BF16)

    def scores(j, buf):
        keys = pl.ds(pl.multiple_of(j * tk, tk), tk)
        for h in heads:
            st = lax.dot_general(k_ref[keys, head_lanes(h)], q_ref[:, head_lanes(h)],
                                 (((1,), (1,)), ((), ())), preferred_element_type=F32)
            s_ref[buf, h] = st
            mx_ref[buf, h] = jnp.max(st, axis=0, keepdims=True)

    def softmax(buf):
        for h in heads:
            m_old = m_ref[h]
            m_new = jnp.maximum(m_old, mx_ref[buf, h])
            al_ref[buf, h] = jnp.exp2(m_old - m_new)
            m_ref[h] = m_new
            p_ref[buf, h] = jnp.exp2(s_ref[buf, h] - m_new).astype(BF16)

    def values(j, buf):
        keys = pl.ds(pl.multiple_of(j * tk, tk), tk)
        for h in heads:
            p = p_ref[buf, h]
            alpha = al_ref[buf, h]
            pv = jnp.dot(vt_ref[head_rows(h), keys], p, preferred_element_type=F32)
            acc_ref[head_rows(h), :] = alpha * acc_ref[head_rows(h), :] + pv
            psum = jnp.dot(ones, p, preferred_element_type=F32)
            l_ref[h] = alpha * l_ref[h] + psum[0:1, :]

    scores(0, 0)

    def pair(i, carry):
        j = 2 * i
        softmax(0)
        scores(j + 1, 1)
        values(jnp.maximum(j - 1, 0), 1)
        softmax(1)
        scores(jnp.minimum(j + 2, nk - 1), 0)
        values(j, 0)
        return carry

    lax.fori_loop(0, nk // 2, pair, 0)
    values(nk - 1, 1)

    inv = jnp.concatenate([jnp.broadcast_to(1.0 / l_ref[h], (MLA_V, tq)) for h in heads], axis=0)
    o_ref[...] = (acc_ref[...] * inv).T


ATTN_MIN_KEY_BLOCKS = 2


def _attention(q, k, vt, *, tq=512, tk=512):
    B, L, _ = q.shape
    tq, tk = min(tq, L), min(tk, L // ATTN_MIN_KEY_BLOCKS)
    pairs = MLA_HEADS // MLA_PAIR
    return pl.pallas_call(
        functools.partial(_attn_kernel, tk=tk),
        grid=(B, pairs, L // tq),
        in_specs=[pl.BlockSpec((None, tq, MLA_PAIR * MLA_HEAD_PAD), lambda b, h, qi: (b, qi, h)),
                  pl.BlockSpec((None, L, MLA_PAIR * MLA_HEAD_PAD), lambda b, h, qi: (b, 0, h)),
                  pl.BlockSpec((None, LANES, L), lambda b, h, qi: (b, h, 0))],
        out_specs=pl.BlockSpec((None, tq, LANES), lambda b, h, qi: (b, qi, h)),
        out_shape=jax.ShapeDtypeStruct((B, L, MLA_V_W), F32),
        scratch_shapes=[pltpu.VMEM((2, MLA_PAIR, tk, tq), F32), pltpu.VMEM((2, MLA_PAIR, tk, tq), BF16),
                        pltpu.VMEM((2, MLA_PAIR, 1, tq), F32), pltpu.VMEM((2, MLA_PAIR, 1, tq), F32),
                        pltpu.VMEM((MLA_PAIR, 1, tq), F32),
                        pltpu.VMEM((MLA_PAIR, 1, tq), F32), pltpu.VMEM((LANES, tq), F32)],
        compiler_params=_cparams(("parallel", "parallel", "parallel")),
        name="attention",
    )(q, k, vt)


def _split3_rows(x):
    hi = x.astype(BF16)
    lo = (x - hi.astype(F32)).astype(BF16)
    return jnp.concatenate([hi, lo, hi], axis=0)


def _split3_cols(m):
    hi = m.astype(BF16)
    lo = (m - hi.astype(F32)).astype(BF16)
    return jnp.concatenate([hi, hi, lo], axis=-1)


def _fft_factor(n):
    n1 = 1 << ((n.bit_length() - 1 + 1) // 2)
    return n1, n // n1


def _dft_tables(L):
    N = 2 * L
    N1, N2 = _fft_factor(N)
    h = N1 // 2
    k1 = jnp.arange(N1, dtype=jnp.int32)
    ang1 = (-2.0 * math.pi / N1) * ((k1[:, None] * k1[None, :]) % N1).astype(F32)
    f1r, f1i = jnp.cos(ang1), jnp.sin(ang1)
    lead_pair = jnp.concatenate([jnp.concatenate([f1r[:, :h], -f1i[:, :h]], axis=1),
                                 jnp.concatenate([f1i[:, :h], f1r[:, :h]], axis=1)], axis=0)
    lead_real = jnp.concatenate([f1r, f1i], axis=0)
    f1rt, f1it = f1r.T[:h], f1i.T[:h]
    lead_inv = jnp.concatenate([jnp.concatenate([f1rt, f1it], axis=1),
                                jnp.concatenate([-f1it, f1rt], axis=1)], axis=0) * (1.0 / N)
    n2 = jnp.arange(N2, dtype=jnp.int32)
    freq = k1[:, None, None] + N1 * n2[None, :, None]
    idx = (freq * n2[None, None, :]) % N
    ang2 = (-2.0 * math.pi / N) * idx.astype(F32)
    gr, gi = jnp.cos(ang2), jnp.sin(ang2)
    inner_fwd = jnp.concatenate([jnp.concatenate([gr, -gi], axis=2),
                                 jnp.concatenate([gi, gr], axis=2)], axis=1)
    grt, git = jnp.swapaxes(gr, 1, 2), jnp.swapaxes(gi, 1, 2)
    inner_inv = jnp.concatenate([jnp.concatenate([grt, git], axis=2),
                                 jnp.concatenate([-git, grt], axis=2)], axis=1)
    return dict(N1=N1, N2=N2, lead_pair=_split3_cols(lead_pair), lead_real=_split3_cols(lead_real),
                lead_inv=_split3_cols(lead_inv), inner_fwd=_split3_cols(inner_fwd),
                inner_inv=_split3_cols(inner_inv))


FFT_LANE_TILE = 1024


def _dft_lead_kernel(m_ref, x_ref, o_ref):
    k = m_ref.shape[1] // 3
    x = x_ref[...].reshape(k, x_ref.shape[-1])
    o_ref[...] = jnp.dot(m_ref[...], _split3_rows(x), preferred_element_type=F32).reshape(o_ref.shape)


def _dft_lead_gate_kernel(m_ref, x_ref, z_ref, g_ref, sk_ref, o_ref):
    k = m_ref.shape[1] // 3
    x = x_ref[...].reshape(k, x_ref.shape[-1])
    conv = jnp.dot(m_ref[...], _split3_rows(x), preferred_element_type=F32).reshape(o_ref.shape)
    o_ref[...] = g_ref[...] * (conv + sk_ref[...] * z_ref[...])


def _dft_inner_conv_kernel(mf_ref, mi_ref, a_ref, kf_ref, o_ref):
    n2 = a_ref.shape[1]
    a = a_ref[...].reshape(2 * n2, a_ref.shape[-1])
    x = jnp.dot(mf_ref[...], _split3_rows(a), preferred_element_type=F32)
    xr, xi = x[:n2], x[n2:]
    kr, ki = kf_ref[0], kf_ref[1]
    y = jnp.concatenate([xr * kr - xi * ki, xr * ki + xi * kr], axis=0)
    o_ref[...] = jnp.dot(mi_ref[...], _split3_rows(y), preferred_element_type=F32).reshape(o_ref.shape)


def _dft_inner_filter_kernel(mf_ref, a_ref, s_ref, o_ref):
    n2 = a_ref.shape[1]
    a = a_ref[...].reshape(2 * n2, a_ref.shape[-1])
    x = jnp.dot(mf_ref[...], _split3_rows(a), preferred_element_type=F32) * s_ref[...]
    o_ref[...] = x.reshape(o_ref.shape)


def _filter_spectrum(kfilt, inv_norm, tabs):
    N, Cf = kfilt.shape
    N1, N2 = tabs['N1'], tabs['N2']
    W = N2 * Cf
    wc = min(FFT_LANE_TILE, W)
    a = pl.pallas_call(
        _dft_lead_kernel,
        grid=(W // wc,),
        in_specs=[pl.BlockSpec((2 * N1, 3 * N1), lambda j: (0, 0)), pl.BlockSpec((N1, wc), lambda j: (0, j))],
        out_specs=pl.BlockSpec((2, N1, wc), lambda j: (0, 0, j)),
        out_shape=jax.ShapeDtypeStruct((2, N1, W), F32),
        compiler_params=_cparams(("parallel",)),
        name="hyena_filter_dft_lead",
    )(tabs['lead_real'], kfilt.reshape(N1, W))
    return pl.pallas_call(
        _dft_inner_filter_kernel,
        grid=(N1,),
        in_specs=[pl.BlockSpec((None, 2 * N2, 6 * N2), lambda k1: (k1, 0, 0)),
                  pl.BlockSpec((2, None, N2, Cf), lambda k1: (0, k1, 0, 0)),
                  pl.BlockSpec((1, Cf), lambda k1: (0, 0))],
        out_specs=pl.BlockSpec((2, None, N2, Cf), lambda k1: (0, k1, 0, 0)),
        out_shape=jax.ShapeDtypeStruct((2, N1, N2, Cf), F32),
        compiler_params=_cparams(("parallel",)),
        name="hyena_filter_dft_inner",
    )(tabs['inner_fwd'], a.reshape(2, N1, N2, Cf), inv_norm)


def _long_conv_gate(z, gate, skip, kspec, order, tabs):
    B, L, C = z.shape
    N1, N2 = tabs['N1'], tabs['N2']
    h = N1 // 2
    P = B // 2
    W = N2 * C
    wc = min(FFT_LANE_TILE, W)
    zv = z.reshape(B, h, W)
    a = pl.pallas_call(
        _dft_lead_kernel,
        grid=(P, W // wc),
        in_specs=[pl.BlockSpec((2 * N1, 3 * N1), lambda p, j: (0, 0)),
                  pl.BlockSpec((2, h, wc), lambda p, j: (p, 0, j))],
        out_specs=pl.BlockSpec((None, 2, N1, wc), lambda p, j: (p, 0, 0, j)),
        out_shape=jax.ShapeDtypeStruct((P, 2, N1, W), F32),
        compiler_params=_cparams(("parallel", "parallel")),
        name="hyena_dft_lead",
    )(tabs['lead_pair'], zv)
    inner = pl.BlockSpec((None, 2, None, N2, C), lambda p, k1: (p, 0, k1, 0, 0))
    mat = pl.BlockSpec((None, 2 * N2, 6 * N2), lambda p, k1: (k1, 0, 0))
    bm = pl.pallas_call(
        _dft_inner_conv_kernel,
        grid=(P, N1),
        in_specs=[mat, mat, inner, pl.BlockSpec((2, None, N2, C), lambda p, k1: (0, k1, 0, order))],
        out_specs=inner,
        out_shape=jax.ShapeDtypeStruct((P, 2, N1, N2, C), F32),
        compiler_params=_cparams(("parallel", "parallel")),
        name="hyena_dft_inner",
    )(tabs['inner_fwd'], tabs['inner_inv'], a.reshape(P, 2, N1, N2, C), kspec)
    pair = pl.BlockSpec((2, h, wc), lambda p, j: (p, 0, j))
    sk = jnp.tile(skip.reshape(1, C), (1, wc // C))
    out = pl.pallas_call(
        _dft_lead_gate_kernel,
        grid=(P, W // wc),
        in_specs=[pl.BlockSpec((N1, 6 * N1), lambda p, j: (0, 0)),
                  pl.BlockSpec((None, 2, N1, wc), lambda p, j: (p, 0, 0, j)), pair, pair,
                  pl.BlockSpec((1, wc), lambda p, j: (0, 0))],
        out_specs=pair,
        out_shape=jax.ShapeDtypeStruct((B, h, W), F32),
        compiler_params=_cparams(("parallel", "parallel")),
        name="hyena_dft_lead_inv",
    )(tabs['lead_inv'], bm.reshape(P, 2, N1, W), zv, gate.reshape(B, h, W), sk)
    return out.reshape(B, L, C)


def _short_conv_kernel(u_ref, up_ref, un_ref, cw_ref, cb_ref, z_out, x1_out, x2_out):
    i = pl.program_id(1)
    u = u_ref[...]
    u_prev, u_next = _shifted(u, up_ref, un_ref, i, pl.num_programs(1))
    cw = cw_ref[...]
    y = u_prev * cw[0:1, :] + u * cw[1:2, :] + u_next * cw[2:3, :] + cb_ref[...]
    z_out[...] = y[:, :HY_W]
    x1_out[...] = y[:, HY_W:2 * HY_W]
    x2_out[...] = y[:, 2 * HY_W:]


def _short_conv(u, conv_w, conv_b, *, tm=512):
    B, L, W = u.shape
    tm = min(tm, L)
    main, prev, nxt = _halo_specs(tm, W, L, lead=0)
    part = pl.BlockSpec((None, tm, HY_W), lambda b, i: (b, i, 0))
    return pl.pallas_call(
        _short_conv_kernel,
        grid=(B, L // tm),
        in_specs=[main, prev, nxt, pl.BlockSpec((3, W), lambda b, i: (0, 0)), pl.BlockSpec((1, W), lambda b, i: (0, 0))],
        out_specs=[part] * 3,
        out_shape=[jax.ShapeDtypeStruct((B, L, HY_W), F32)] * 3,
        compiler_params=_cparams(("parallel", "parallel")),
        name="hyena_short_conv",
    )(u, u, u, conv_w, conv_b.reshape(1, W))


HY_FILT_HID = 64
HY_FILT_OUT = 2 * HY_ORDER * HY_W
TWO_PI = 2.0 * math.pi


def _hdot(a, b):
    return jnp.dot(a, b, precision=lax.Precision.HIGHEST, preferred_element_type=F32)


def _filter_mlp_kernel(bands_ref, w1t_ref, w1s_ref, w1c_ref, b1_ref, w2_ref, b2_ref, w3_ref, fr_ref, dec_ref,
                       h_out, abs_out, *, seq_len):
    i = pl.program_id(0)
    tm = h_out.shape[0]
    row = i * tm + lax.broadcasted_iota(jnp.int32, (tm, 1), 0)
    t = row.astype(F32) / seq_len
    ph = (TWO_PI * t) * bands_ref[...]
    fr = fr_ref[...]
    pre1 = t * w1t_ref[...] + _hdot(jnp.sin(ph), w1s_ref[...]) + _hdot(jnp.cos(ph), w1c_ref[...]) + b1_ref[...]
    h1 = jnp.sin(fr * pre1)
    h2 = jnp.sin(fr * (_hdot(h1, w2_ref[...]) + b2_ref[...]))
    hf = _hdot(h2, w3_ref[...]) * jnp.exp(-t * jnp.abs(dec_ref[...]))
    h_out[...] = hf

    @pl.when(i == 0)
    def _():
        abs_out[...] = jnp.zeros_like(abs_out)

    col = lax.broadcasted_iota(jnp.int32, (1, HY_FILT_OUT), 1)
    used = jnp.where(jnp.logical_and(row == 0, col >= HY_FILT_OUT // 2), 0.0, jnp.abs(hf))
    abs_out[...] += jnp.sum(used, axis=0, keepdims=True)


def _hyena_filter_spectrum(L, w1, b1, w2, b2, w3, freq, decay, tabs, *, tm=256):
    tm = min(tm, L)
    bands = _pad_axis(jnp.linspace(1e-4, HY_BANDS - 1, HY_BANDS, dtype=F32).reshape(1, HY_BANDS), 1, LANES)
    w1p = _pad_axis(w1, 1, LANES)
    w1t = w1p[0:1]
    w1s = _pad_axis(w1p[1:1 + HY_BANDS], 0, LANES)
    w1c = _pad_axis(w1p[1 + HY_BANDS:], 0, LANES)
    pad_vec = lambda v: _pad_axis(v.reshape(1, -1), 1, LANES)
    w2p = _pad_axis(_pad_axis(w2, 0, LANES), 1, LANES)
    w3p = _pad_axis(w3, 0, LANES)
    full = lambda a: pl.BlockSpec(a.shape, lambda i: (0, 0))
    args = (bands, w1t, w1s, w1c, pad_vec(b1), w2p, pad_vec(b2), w3p, pad_vec(freq), decay.reshape(1, HY_FILT_OUT))
    hf, abs_sum = pl.pallas_call(
        functools.partial(_filter_mlp_kernel, seq_len=L),
        grid=(L // tm,),
        in_specs=[full(a) for a in args],
        out_specs=[pl.BlockSpec((tm, HY_FILT_OUT), lambda i: (i, 0)), pl.BlockSpec((1, HY_FILT_OUT), lambda i: (0, 0))],
        out_shape=[jax.ShapeDtypeStruct((L, HY_FILT_OUT), F32), jax.ShapeDtypeStruct((1, HY_FILT_OUT), F32)],
        compiler_params=_cparams(("arbitrary",)),
        name="hyena_filter_mlp",
    )(*args)
    half = HY_FILT_OUT // 2
    kfilt = jnp.concatenate([hf[:, :half], jnp.zeros((1, half), F32), jnp.flip(hf[1:, half:], axis=0)], axis=0)
    inv_norm = 1.0 / (abs_sum[:, :half] + abs_sum[:, half:] + 1e-6)
    return _filter_spectrum(kfilt, inv_norm, tabs)


def _hyena(u, conv_w, conv_b, skip, kspec, tabs):
    z, x1, x2 = _short_conv(u, conv_w, conv_b)
    z = _long_conv_gate(z, x1, skip[0], kspec, 0, tabs)
    return _long_conv_gate(z, x2, skip[1], kspec, 1, tabs)


def _mix_out_ln_kernel(a_ref, b_ref, wa_ref, wb_ref, res_ref, g_ref, bb_ref, out_ref):
    m = (jnp.dot(a_ref[...].astype(BF16), wa_ref[...], preferred_element_type=F32)
         + jnp.dot(b_ref[...].astype(BF16), wb_ref[...], preferred_element_type=F32))
    out_ref[...] = _layer_norm_rows(DN_ALPHA * res_ref[...] + m, g_ref[...], bb_ref[...])


def _mix_out_ln(y_hy, y_mla, w_out, res, ln_g, ln_b, *, tm=512):
    T, D = res.shape
    tm = min(tm, T)
    wa = w_out[:HY_W].astype(BF16)
    wb = w_out[HY_W:].astype(BF16)
    row = lambda w: pl.BlockSpec((tm, w), lambda i: (i, 0))
    full = lambda a: pl.BlockSpec(a.shape, lambda i: (0, 0))
    vec = pl.BlockSpec((1, D), lambda i: (0, 0))
    return pl.pallas_call(
        _mix_out_ln_kernel,
        grid=(T // tm,),
        in_specs=[row(HY_W), row(MLA_V_W), full(wa), full(wb), row(D), vec, vec],
        out_specs=row(D),
        out_shape=jax.ShapeDtypeStruct((T, D), F32),
        compiler_params=_cparams(("parallel",)),
        name="mix_out_ln",
    )(y_hy, y_mla, wa, wb, res, ln_g.reshape(1, D), ln_b.reshape(1, D))


def _hyena_mla_layer(x, p, kspec, tabs, ln_g, ln_b):
    B, L, D = x.shape
    u, q, k, v = _in_proj(x, p['w_in'], p['w_uq'], p['w_ukv'], p['q_norm'], p['kv_norm'])
    y_hy = _hyena(u, p['conv_w'], p['conv_b'], p['skip'], kspec, tabs)
    y_mla = _attention(q, k, v)
    return _mix_out_ln(y_hy.reshape(B * L, HY_W), y_mla.reshape(B * L, MLA_V_W), p['w_out'],
                       x.reshape(B * L, D), ln_g, ln_b).reshape(B, L, D)


MOE_ROWS = 256
ROUTE_W = LANES
NEG_BIG = -1e30


def _router_kernel(x_ref, w_ref, idx_out, gate_out):
    logits = _hdot(x_ref[...], w_ref[...])
    lane = lax.broadcasted_iota(jnp.int32, logits.shape, 1)
    is_group = lane < MOE_GROUPS
    gl = jnp.where(is_group, logits, NEG_BIG)
    gmax = jnp.max(gl, axis=-1, keepdims=True)
    g_top = jnp.min(jnp.where(gl == gmax, lane, ROUTE_W), axis=-1, keepdims=True)
    g_prob = 1.0 / jnp.sum(jnp.where(is_group, jnp.exp(gl - gmax), 0.0), axis=-1, keepdims=True)

    first = MOE_GROUPS + g_top * MOE_PER_GROUP
    in_grp = jnp.logical_and(lane >= first, lane < first + MOE_PER_GROUP)
    el = jnp.where(in_grp, logits, NEG_BIG)
    emax = jnp.max(el, axis=-1, keepdims=True)
    ex = jnp.where(in_grp, jnp.exp(el - emax), 0.0)
    prob = ex / jnp.sum(ex, axis=-1, keepdims=True)
    pm = jnp.where(in_grp, prob, -1.0)
    p1 = jnp.max(pm, axis=-1, keepdims=True)
    l1 = jnp.min(jnp.where(pm == p1, lane, ROUTE_W), axis=-1, keepdims=True)
    pm2 = jnp.where(lane == l1, -1.0, pm)
    p2 = jnp.max(pm2, axis=-1, keepdims=True)
    l2 = jnp.min(jnp.where(pm2 == p2, lane, ROUTE_W), axis=-1, keepdims=True)
    tot = p1 + p2
    e1 = l1 - MOE_GROUPS
    e2 = l2 - MOE_GROUPS
    idx_out[...] = jnp.where(lane == 0, e1, jnp.where(lane == 1, e2, 0))
    gate_out[...] = jnp.where(lane == 0, g_prob * (p1 / tot), jnp.where(lane == 1, g_prob * (p2 / tot), 0.0))


def _router(xt, w_group, w_expert, *, tm=512):
    T, D = xt.shape
    tm = min(tm, T)
    w = _pad_axis(jnp.concatenate([w_group, w_expert], axis=1), 1, ROUTE_W)
    row = pl.BlockSpec((tm, ROUTE_W), lambda i: (i, 0))
    return pl.pallas_call(
        _router_kernel,
        grid=(T // tm,),
        in_specs=[pl.BlockSpec((tm, D), lambda i: (i, 0)), pl.BlockSpec((D, ROUTE_W), lambda i: (0, 0))],
        out_specs=[row, row],
        out_shape=[jax.ShapeDtypeStruct((T, ROUTE_W), jnp.int32), jax.ShapeDtypeStruct((T, ROUTE_W), F32)],
        compiler_params=_cparams(("parallel",)),
        name="moe_router",
    )(xt, w)


def _gather_rows(src_hbm, idx_ref, idx_base, buf, sem, n_rows):
    def row_copy(r):
        return pltpu.make_async_copy(src_hbm.at[pl.ds(idx_ref[0, idx_base + r], 1)], buf.at[pl.ds(r, 1)], sem)

    def start(r, c):
        row_copy(r).start()
        return c

    def wait(r, c):
        row_copy(r).wait()
        return c

    lax.fori_loop(0, n_rows, start, 0, unroll=8)
    lax.fori_loop(0, n_rows, wait, 0, unroll=8)


def _expert_kernel(blk_e_ref, n_used_ref, tok_ref, x_hbm, w1_ref, w3_ref, w2_ref, o_ref, xbuf, sem):
    i = pl.program_id(0)

    @pl.when(i < n_used_ref[0])
    def _():
        _gather_rows(x_hbm, tok_ref, 0, xbuf, sem, MOE_ROWS)
        xb = xbuf[...].astype(BF16)
        h1 = jnp.dot(xb, w1_ref[...], preferred_element_type=F32)
        h3 = jnp.dot(xb, w3_ref[...], preferred_element_type=F32)
        h = (h1 * jax.nn.sigmoid(h1) * h3).astype(BF16)
        o_ref[...] = jnp.dot(h, w2_ref[...], preferred_element_type=F32)

    @pl.when(i >= n_used_ref[0])
    def _():
        o_ref[...] = jnp.zeros_like(o_ref)


def _experts(xt, src_tok, blk_e, n_used, w1, w3, w2):
    T, D = xt.shape
    n_blocks = blk_e.shape[0]
    wspec = lambda shape: pl.BlockSpec((None,) + shape, lambda i, be, nu: (be[i], 0, 0))
    return pl.pallas_call(
        _expert_kernel,
        grid_spec=pltpu.PrefetchScalarGridSpec(
            num_scalar_prefetch=2,
            grid=(n_blocks,),
            in_specs=[pl.BlockSpec((None, 1, MOE_ROWS), lambda i, be, nu: (i, 0, 0), memory_space=pltpu.SMEM),
                      pl.BlockSpec(memory_space=pl.ANY),
                      wspec((D, MOE_FF)), wspec((D, MOE_FF)), wspec((MOE_FF, D))],
            out_specs=pl.BlockSpec((MOE_ROWS, D), lambda i, be, nu: (i, 0)),
            scratch_shapes=[pltpu.VMEM((MOE_ROWS, D), F32), pltpu.SemaphoreType.DMA(())],
        ),
        out_shape=jax.ShapeDtypeStruct((n_blocks * MOE_ROWS, D), F32),
        compiler_params=_cparams(("arbitrary",)),
        name="moe_experts",
    )(blk_e, n_used, src_tok.reshape(n_blocks, 1, MOE_ROWS), xt, w1, w3, w2)


def _combine_ln_kernel(dest_ref, eo_hbm, gate_ref, x_ref, g_ref, b_ref, out_ref, buf0, buf1, sems):
    tm = x_ref.shape[0]
    _gather_rows(eo_hbm, dest_ref, 0, buf0, sems.at[0], tm)
    _gather_rows(eo_hbm, dest_ref, tm, buf1, sems.at[1], tm)
    gate = gate_ref[...]
    f = gate[:, 0:1] * buf0[...] + gate[:, 1:2] * buf1[...]
    out_ref[...] = _layer_norm_rows(DN_ALPHA * x_ref[...] + f, g_ref[...], b_ref[...])


def _combine_ln(eo, dest, gate, xt, ln_g, ln_b, *, tm=256):
    T, D = xt.shape
    tm = min(tm, T)
    nt = T // tm
    dest_t = dest.reshape(nt, tm, MOE_TOPK).transpose(0, 2, 1).reshape(nt, 1, MOE_TOPK * tm)
    row = pl.BlockSpec((tm, D), lambda i: (i, 0))
    vec = pl.BlockSpec((1, D), lambda i: (0, 0))
    return pl.pallas_call(
        _combine_ln_kernel,
        grid=(nt,),
        in_specs=[pl.BlockSpec((None, 1, MOE_TOPK * tm), lambda i: (i, 0, 0), memory_space=pltpu.SMEM),
                  pl.BlockSpec(memory_space=pl.ANY), pl.BlockSpec((tm, ROUTE_W), lambda i: (i, 0)), row, vec, vec],
        out_specs=row,
        out_shape=jax.ShapeDtypeStruct((T, D), F32),
        scratch_shapes=[pltpu.VMEM((tm, D), F32), pltpu.VMEM((tm, D), F32), pltpu.SemaphoreType.DMA((2,))],
        compiler_params=_cparams(("arbitrary",)),
        name="moe_combine_ln",
    )(dest_t, eo, gate, xt, ln_g.reshape(1, D), ln_b.reshape(1, D))


def _dispatch_plan(expert):
    T = expert.shape[0]
    A = T * MOE_TOPK
    n_blocks = A // MOE_ROWS + MOE_EXPERTS
    flat_e = expert.reshape(-1)
    order = jnp.argsort(flat_e, stable=True)
    e_sorted = flat_e[order]
    counts = jnp.bincount(flat_e, length=MOE_EXPERTS)
    padded = (counts + MOE_ROWS - 1) // MOE_ROWS * MOE_ROWS
    pad_end = jnp.cumsum(padded)
    pad_start = pad_end - padded
    start = jnp.cumsum(counts) - counts
    dest_sorted = (pad_start[e_sorted] + jnp.arange(A) - start[e_sorted]).astype(jnp.int32)
    dest = jnp.zeros((A,), jnp.int32).at[order].set(dest_sorted)
    src_tok = jnp.zeros((n_blocks * MOE_ROWS,), jnp.int32).at[dest_sorted].set((order // MOE_TOPK).astype(jnp.int32))
    blk_e = jnp.minimum(jnp.searchsorted(pad_end, jnp.arange(n_blocks) * MOE_ROWS, side='right'),
                        MOE_EXPERTS - 1).astype(jnp.int32)
    n_used = (pad_end[-1] // MOE_ROWS).astype(jnp.int32).reshape(1)
    return dest.reshape(T, MOE_TOPK), src_tok, blk_e, n_used


def _moe_layer(x, w_group, w_expert, w1, w3, w2, ln_g, ln_b):
    B, L, D = x.shape
    xt = x.reshape(B * L, D)
    idx, gate = _router(xt, w_group, w_expert)
    dest, src_tok, blk_e, n_used = _dispatch_plan(idx[:, :MOE_TOPK])
    eo = _experts(xt, src_tok, blk_e, n_used, w1, w3, w2)
    return _combine_ln(eo, dest, gate, xt, ln_g, ln_b).reshape(B, L, D)


def _trunk(x, p, tabs, kspecs):
    for layer in range(DEPTH):
        i = layer // 2
        if layer % 2 == 0:
            lp = {n: p[n][i] for n in ('w_in', 'w_uq', 'w_ukv', 'q_norm', 'kv_norm', 'conv_w', 'conv_b', 'skip', 'w_out')}
            x = _hyena_mla_layer(x, lp, kspecs[i], tabs, p['ln1_g'][layer], p['ln1_b'][layer])
        else:
            lp = {n[3:]: p[n][i] for n in p if n.startswith('rw_')}
            x = _rwkv_layer(x, lp, p['ln1_g'][layer], p['ln1_b'][layer])
        x = _moe_layer(x, p['moe_w_group'][layer], p['moe_w_expert'][layer], p['moe_w1'][layer],
                       p['moe_w3'][layer], p['moe_w2'][layer], p['ln2_g'][layer], p['ln2_b'][layer])
    return x


def kernel(x_prompt, x_sample, ln1_g, ln1_b, ln2_g, ln2_b, mix_w_in, hy_conv_w, hy_conv_b, hy_ffn_w1,
           hy_ffn_b1, hy_ffn_w2, hy_ffn_b2, hy_ffn_w3, hy_sin_freq, hy_decay, hy_skip, mla_q_norm, mla_w_uq,
           mla_kv_norm, mla_w_ukv, mix_w_out, rw_mu, rw_w_r, rw_w_k, rw_w_v, rw_w0, rw_w1, rw_w2, rw_a0,
           rw_a1, rw_a2, rw_g1, rw_g2, rw_k_k, rw_k_a, rw_r_k, rw_gn_g, rw_gn_b, rw_w_o, moe_w_group,
           moe_w_expert, moe_w1, moe_w3, moe_w2):
    p = dict(ln1_g=ln1_g, ln1_b=ln1_b, ln2_g=ln2_g, ln2_b=ln2_b, w_in=mix_w_in, conv_w=hy_conv_w,
             conv_b=hy_conv_b, skip=hy_skip, q_norm=mla_q_norm, w_uq=mla_w_uq, kv_norm=mla_kv_norm,
             w_ukv=mla_w_ukv, w_out=mix_w_out, rw_mu=rw_mu, rw_w_r=rw_w_r, rw_w_k=rw_w_k, rw_w_v=rw_w_v,
             rw_w0=rw_w0, rw_w1=rw_w1, rw_w2=rw_w2, rw_a0=rw_a0, rw_a1=rw_a1, rw_a2=rw_a2, rw_g1=rw_g1,
             rw_g2=rw_g2, rw_k_k=rw_k_k, rw_k_a=rw_k_a, rw_r_k=rw_r_k, rw_gn_g=rw_gn_g, rw_gn_b=rw_gn_b,
             rw_w_o=rw_w_o, moe_w_group=moe_w_group, moe_w_expert=moe_w_expert,
             moe_w1=moe_w1.astype(BF16), moe_w3=moe_w3.astype(BF16), moe_w2=moe_w2.astype(BF16))
    outs = []
    for x in (x_prompt, x_sample):
        L = x.shape[1]
        tabs = _dft_tables(L)
        kspecs = [_hyena_filter_spectrum(L, hy_ffn_w1[i], hy_ffn_b1[i], hy_ffn_w2[i], hy_ffn_b2[i],
                                         hy_ffn_w3[i], hy_sin_freq[i], hy_decay[i], tabs)
                  for i in range((DEPTH + 1) // 2)]
        outs.append(_trunk(x, p, tabs, kspecs))
    return tuple(outs)
```

```python
import functools
import math

import jax
import jax.numpy as jnp
import numpy as np
from jax import lax
from jax.experimental import pallas as pl
from jax.experimental.pallas import tpu as pltpu

F32 = jnp.float32
BF16 = jnp.bfloat16

LANES = 128
VMEM_LIMIT_BYTES = 56 * 1024 * 1024

D_MODEL = 1024
DEPTH = 4
HY_W = D_MODEL // 2
HY_ORDER = 2
HY_BANDS = 16
MLA_HEADS = 8
MLA_NOPE = 64
MLA_ROPE = 32
MLA_V = 64
MLA_Q_LORA = 256
MLA_KV_LORA = 128
ROPE_THETA = 10000.0
RW_HEAD = 64
RW_HEADS = D_MODEL // RW_HEAD
RW_GN_EPS = 64e-5
MOE_GROUPS = 4
MOE_PER_GROUP = 8
MOE_EXPERTS = MOE_GROUPS * MOE_PER_GROUP
MOE_TOPK = 2
MOE_FF = 512
LN_EPS = 1e-5
RMS_EPS = 1e-6
DN_ALPHA = (2 * DEPTH) ** 0.25

RW_CHUNK = 64
HEADS_PER_SLAB = LANES // RW_HEAD


def _cparams(sem):
    return pltpu.CompilerParams(dimension_semantics=sem, vmem_limit_bytes=VMEM_LIMIT_BYTES)


def _run_staged(chains, skew):
    done = [False] * len(chains)
    rnd = 0
    while not all(done):
        for c, ch in enumerate(chains):
            if rnd >= skew * c and not done[c]:
                done[c] = not next(ch, False)
        rnd += 1


def _bdot(a, b):
    return jnp.dot(a.astype(BF16), b.astype(BF16), preferred_element_type=F32)


def _bdot_nt(a, b):
    return lax.dot_general(a.astype(BF16), b.astype(BF16), (((1,), (1,)), ((), ())),
                           preferred_element_type=F32)


def _bdot_tn(a, b):
    return lax.dot_general(a.astype(BF16), b.astype(BF16), (((0,), (0,)), ((), ())),
                           preferred_element_type=F32)


def _rwkv_scan_kernel(r_ref, lw_ref, k_ref, v_ref, kk_ref, b_ref, g_ref, rk_ref, gng_ref, gnb_ref,
                      o_ref, s_ref, *, tb):
    C = RW_CHUNK
    z = pl.program_id(0)
    fwd = z == 0

    @pl.when(pl.program_id(3) == 0)
    def _():
        s_ref[...] = jnp.zeros_like(s_ref)

    lane = lax.broadcasted_iota(jnp.int32, (1, LANES), 1)
    m0 = lane < RW_HEAD
    sgn = jnp.where(fwd, 1, -1)
    ri = lax.broadcasted_iota(jnp.int32, (C, C), 0)
    ci = lax.broadcasted_iota(jnp.int32, (C, C), 1)
    tri_c = jnp.where((ci - ri) * sgn <= 0, 1.0, 0.0).astype(BF16)
    r2 = lax.broadcasted_iota(jnp.int32, (2 * C, 2 * C), 0)
    c2 = lax.broadcasted_iota(jnp.int32, (2 * C, 2 * C), 1)
    head_r = jnp.where(r2 >= C, 1, 0)
    head_c = jnp.where(c2 >= C, 1, 0)
    lag = jnp.where(head_r == head_c, (c2 - r2) * sgn, 1)
    strict = lag < 0
    incl = lag <= 0

    def stack(x):
        return jnp.concatenate([jnp.where(m0, x, 0.0), jnp.where(m0, 0.0, x)], axis=0)

    def head_sum(x):
        s0 = jnp.sum(jnp.where(m0, x, 0.0), axis=-1, keepdims=True)
        s1 = jnp.sum(jnp.where(m0, 0.0, x), axis=-1, keepdims=True)
        return jnp.where(m0, s0, s1)

    def chunk(j, carry):
        start = jnp.where(fwd, j * C, tb - (j + 1) * C)
        start = pl.multiple_of(start, C)
        sl = pl.ds(start, C)
        _run_staged([slab_chunk(sl, slab) for slab in range(s_ref.shape[0])], skew=0)
        return carry

    def slab_chunk(sl, slab):
        ln = slice(slab * LANES, (slab + 1) * LANES)
        rk = rk_ref[:, ln]
        gng = gng_ref[:, ln]
        gnb = gnb_ref[:, ln]
        r = r_ref[sl, ln]
        lw = lw_ref[sl, ln]
        k = k_ref[sl, ln]
        v = v_ref[sl, ln]
        kk = kk_ref[sl, ln]
        b = b_ref[sl, ln]
        g = g_ref[sl, ln]
        S = s_ref[slab]

        lw_hi = lw.astype(BF16)
        rem = lw - lw_hi.astype(F32)
        lw_mid = rem.astype(BF16)
        lw_lo = (rem - lw_mid.astype(F32)).astype(BF16)
        cl3 = jnp.dot(tri_c, jnp.concatenate([lw_hi, lw_mid, lw_lo], axis=1), preferred_element_type=F32)
        cl = cl3[:, :LANES] + cl3[:, LANES:2 * LANES] + cl3[:, 2 * LANES:]
        yield True
        cl_last = jnp.where(fwd, cl[C - 1:C, :], cl[0:1, :])
        eneg = jnp.exp(-cl)
        epos = jnp.exp(cl)
        ehat = jnp.exp(cl_last - cl)
        at = -kk * jnp.exp(cl - lw)
        la_lr = jnp.concatenate([stack(at), stack(r * epos)], axis=0).astype(BF16)
        rb_rk = jnp.concatenate([stack(b * eneg), stack(k * eneg)], axis=0).astype(BF16)
        v2 = stack(v)

        sc = _bdot_nt(la_lr, rb_rk)
        hs = _bdot_nt(la_lr, S)
        ys = hs[2 * C:]
        yield True
        mab = jnp.where(strict, sc[:2 * C, :2 * C], 0.0)
        mak = jnp.where(strict, sc[:2 * C, 2 * C:], 0.0)
        nrb = jnp.where(incl, sc[2 * C:, :2 * C], 0.0)
        nrk = jnp.where(incl, sc[2 * C:, 2 * C:], 0.0)
        x = hs[:2 * C] + _bdot(mak, v2)
        mp = mab
        yield True
        x = x + _bdot(mp, x)
        for _ in range(int(math.log2(C)) - 1):
            mp = _bdot(mp, mp)
            yield True
            x = x + _bdot(mp, x)
        yield True
        uv = jnp.concatenate([x, v2], axis=0)
        y2 = ys + _bdot(jnp.concatenate([nrb, nrk], axis=1), uv)
        bh_kh = jnp.concatenate([stack(b * ehat), stack(k * ehat)], axis=0)
        s_ref[slab] = S * jnp.exp(cl_last) + _bdot_tn(uv, bh_kh)
        yield True
        y = y2[:C] + y2[C:]

        mean = head_sum(y) * (1.0 / RW_HEAD)
        d = y - mean
        var = head_sum(d * d) * (1.0 / RW_HEAD)
        yn = d * lax.rsqrt(var + RW_GN_EPS) * gng + gnb
        bonus = head_sum(r * k * rk) * v
        o_ref[sl, ln] = (yn + bonus) * g

    lax.fori_loop(0, tb // C, chunk, 0)


RW_SLABS_PER_STEP = 8


def _rwkv_scan(r, lw, k, v, kk, b, g, r_k, gn_g, gn_b, *, tb=512):
    Z, B, L, D = r.shape
    tb = min(tb, L)
    nblk = L // tb
    ns = min(RW_SLABS_PER_STEP, D // LANES)
    width = ns * LANES

    def amap(z, bb, h, i):
        return (z, bb, jnp.where(z == 0, i, nblk - 1 - i), h)

    act = pl.BlockSpec((None, None, tb, width), amap)
    par = pl.BlockSpec((1, width), lambda z, bb, h, i: (0, h))
    return pl.pallas_call(
        functools.partial(_rwkv_scan_kernel, tb=tb),
        grid=(Z, B, D // width, nblk),
        in_specs=[act] * 7 + [par] * 3,
        out_specs=act,
        out_shape=jax.ShapeDtypeStruct((Z, B, L, D), F32),
        scratch_shapes=[pltpu.VMEM((ns, LANES, LANES), F32)],
        compiler_params=_cparams(("parallel", "parallel", "parallel", "arbitrary")),
        name="rwkv_scan",
    )(r, lw, k, v, kk, b, g, r_k.reshape(1, D), gn_g.reshape(1, D), gn_b.reshape(1, D))


HALO_ROWS = 8


def _halo_specs(tm, width, seq_len, lead):
    per = tm // HALO_ROWS
    last = seq_len // HALO_ROWS - 1

    def main(*g):
        return (g[lead], g[lead + 1], 0)

    def prev(*g):
        return (g[lead], jnp.maximum(g[lead + 1] * per - 1, 0), 0)

    def nxt(*g):
        return (g[lead], jnp.minimum((g[lead + 1] + 1) * per, last), 0)

    return (pl.BlockSpec((None, tm, width), main), pl.BlockSpec((None, HALO_ROWS, width), prev),
            pl.BlockSpec((None, HALO_ROWS, width), nxt))


def _shifted(x, prev_ref, next_ref, i, n_tiles):
    tm = x.shape[0]
    prev_row = jnp.where(i == 0, 0.0, prev_ref[HALO_ROWS - 1:HALO_ROWS, :])
    next_row = jnp.where(i == n_tiles - 1, 0.0, next_ref[0:1, :])
    row = lax.broadcasted_iota(jnp.int32, (tm, 1), 0)
    x_prev = jnp.where(row == 0, prev_row, pltpu.roll(x, 1, axis=0))
    x_next = jnp.where(row == tm - 1, next_row, pltpu.roll(x, tm - 1, axis=0))
    return x_prev, x_next


def _layer_norm_rows(y, g, b):
    mu = jnp.mean(y, axis=-1, keepdims=True)
    d = y - mu
    var = jnp.mean(d * d, axis=-1, keepdims=True)
    return d * lax.rsqrt(var + LN_EPS) * g + b


RW_LORA_PAD = 128
RW_GATE_PAD = 256
RW_DECAY_SCALE = math.exp(-0.5)


def _rwkv_proj_kernel(x_ref, xp_ref, xn_ref, mu_ref, wr_ref, wk_ref, wv_ref, w0_ref, w1_ref, w2_ref,
                      a0_ref, a1_ref, a2_ref, g1_ref, g2_ref, kk_ref, ka_ref,
                      r_out, lw_out, k_out, v_out, kk_out, b_out, g_out):
    z = pl.program_id(0)
    i = pl.program_id(2)
    x = x_ref[...]
    x_prev, x_next = _shifted(x, xp_ref, xn_ref, i, pl.num_programs(2))
    xx = jnp.where(z == 0, x_prev, x_next) - x
    mu = mu_ref[...]

    def mix(j):
        return (x + xx * mu[j:j + 1, :]).astype(BF16)

    r = jnp.dot(mix(0), wr_ref[...], preferred_element_type=F32)
    k = jnp.dot(mix(2), wk_ref[...], preferred_element_type=F32)
    v = jnp.dot(mix(3), wv_ref[...], preferred_element_type=F32)
    lw = w0_ref[...] + _bdot(jnp.tanh(jnp.dot(mix(1), w1_ref[...], preferred_element_type=F32)), w2_ref[...])
    logw = -jax.nn.sigmoid(lw) * RW_DECAY_SCALE
    a = jax.nn.sigmoid(a0_ref[...] + _bdot(jnp.dot(mix(4), a1_ref[...], preferred_element_type=F32), a2_ref[...]))
    g = _bdot(jax.nn.sigmoid(jnp.dot(mix(5), g1_ref[...], preferred_element_type=F32)), g2_ref[...])

    kkr = k * kk_ref[...]
    lane = lax.broadcasted_iota(jnp.int32, (1, LANES), 1)
    m0 = lane < RW_HEAD
    slabs = []
    for s in range(x.shape[1] // LANES):
        q = kkr[:, s * LANES:(s + 1) * LANES]
        q2 = q * q
        n0 = jnp.sum(jnp.where(m0, q2, 0.0), axis=-1, keepdims=True)
        n1 = jnp.sum(jnp.where(m0, 0.0, q2), axis=-1, keepdims=True)
        nrm = jnp.maximum(jnp.sqrt(jnp.where(m0, n0, n1)), 1e-12)
        slabs.append(q / nrm)
    kk = jnp.concatenate(slabs, axis=1)

    r_out[...] = r
    lw_out[...] = logw
    k_out[...] = k * (1.0 + (a - 1.0) * ka_ref[...])
    v_out[...] = v
    kk_out[...] = kk
    b_out[...] = kk * a
    g_out[...] = g


def _pad_axis(w, axis, size):
    pad = [(0, 0)] * w.ndim
    pad[axis] = (0, size - w.shape[axis])
    return jnp.pad(w, pad)


def _rwkv_proj(x, p, *, tm=256):
    B, L, D = x.shape
    tm = min(tm, L)
    main, prev, nxt = _halo_specs(tm, D, L, lead=1)
    full2 = lambda shape: pl.BlockSpec(shape, lambda z, b, i: (0, 0))
    perz = lambda shape: pl.BlockSpec((None,) + shape, lambda z, b, i: (z, 0, 0))
    out_spec = pl.BlockSpec((None, None, tm, D), lambda z, b, i: (z, b, i, 0))
    w1 = _pad_axis(p['w1'], 2, RW_LORA_PAD).astype(BF16)
    w2 = _pad_axis(p['w2'], 1, RW_LORA_PAD).astype(BF16)
    a1 = _pad_axis(p['a1'], 2, RW_LORA_PAD).astype(BF16)
    a2 = _pad_axis(p['a2'], 1, RW_LORA_PAD).astype(BF16)
    g1 = _pad_axis(p['g1'], 2, RW_GATE_PAD).astype(BF16)
    g2 = _pad_axis(p['g2'], 1, RW_GATE_PAD).astype(BF16)
    outs = pl.pallas_call(
        _rwkv_proj_kernel,
        grid=(2, B, L // tm),
        in_specs=[main, prev, nxt, perz((6, D)), full2((D, D)), full2((D, D)), full2((D, D)),
                  perz((1, D)), perz((D, RW_LORA_PAD)), perz((RW_LORA_PAD, D)),
                  perz((1, D)), perz((D, RW_LORA_PAD)), perz((RW_LORA_PAD, D)),
                  perz((D, RW_GATE_PAD)), perz((RW_GATE_PAD, D)), full2((1, D)), full2((1, D))],
        out_specs=[out_spec] * 7,
        out_shape=[jax.ShapeDtypeStruct((2, B, L, D), F32)] * 7,
        compiler_params=_cparams(("parallel", "parallel", "parallel")),
        name="rwkv_proj",
    )(x, x, x, p['mu'], p['w_r'].astype(BF16), p['w_k'].astype(BF16), p['w_v'].astype(BF16),
      p['w0'].reshape(2, 1, D), w1, w2, p['a0'].reshape(2, 1, D), a1, a2, g1, g2,
      p['k_k'].reshape(1, D), p['k_a'].reshape(1, D))
    return outs


def _rw_out_ln_kernel(o_ref, w_ref, res_ref, g_ref, b_ref, out_ref):
    s = (o_ref[0] + o_ref[1]).astype(BF16)
    m = jnp.dot(s, w_ref[...], preferred_element_type=F32)
    out_ref[...] = _layer_norm_rows(DN_ALPHA * res_ref[...] + m, g_ref[...], b_ref[...])


def _rw_out_ln(o, w_o, res, ln_g, ln_b, *, tm=512):
    _, T, D = o.shape
    tm = min(tm, T)
    row = pl.BlockSpec((tm, D), lambda i: (i, 0))
    vec = pl.BlockSpec((1, D), lambda i: (0, 0))
    return pl.pallas_call(
        _rw_out_ln_kernel,
        grid=(T // tm,),
        in_specs=[pl.BlockSpec((2, tm, D), lambda i: (0, i, 0)), pl.BlockSpec((D, D), lambda i: (0, 0)),
                  row, vec, vec],
        out_specs=row,
        out_shape=jax.ShapeDtypeStruct((T, D), F32),
        compiler_params=_cparams(("parallel",)),
        name="rw_out_ln",
    )(o, w_o.astype(BF16), res, ln_g.reshape(1, D), ln_b.reshape(1, D))


def _rwkv_layer(x, p, ln_g, ln_b):
    B, L, D = x.shape
    r, lw, k, v, kk, b, g = _rwkv_proj(x, p)
    o = _rwkv_scan(r, lw, k, v, kk, b, g, p['r_k'].reshape(D), p['gn_g'], p['gn_b'])
    return _rw_out_ln(o.reshape(2, B * L, D), p['w_o'], x.reshape(B * L, D), ln_g, ln_b).reshape(B, L, D)


HY_U_W = (HY_ORDER + 1) * HY_W
MLA_HEAD_PAD = LANES
MLA_QK_W = MLA_HEADS * MLA_HEAD_PAD
MLA_V_W = MLA_HEADS * MLA_V
MLA_SCALE = (MLA_NOPE + MLA_ROPE) ** -0.5 * math.log2(math.e)
HALF_ROPE = MLA_ROPE // 2


def _rms_rows(c, g):
    return c * lax.rsqrt(jnp.mean(c * c, axis=-1, keepdims=True) + RMS_EPS) * g


def _in_proj_kernel(x_ref, wu_ref, ws_ref, wq_ref, wkv_ref, qn_ref, kvn_ref, cos_ref, sin_ref,
                    u_out, q_out, k_out, v_out):
    xb = x_ref[...].astype(BF16)
    u_out[...] = jnp.dot(xb, wu_ref[...], preferred_element_type=F32)
    small = jnp.dot(xb, ws_ref[...], preferred_element_type=F32)
    c_q = small[:, :MLA_Q_LORA]
    c_kv = small[:, MLA_Q_LORA:MLA_Q_LORA + MLA_KV_LORA]
    kpe_a = small[:, MLA_Q_LORA + MLA_KV_LORA:MLA_Q_LORA + MLA_KV_LORA + LANES]
    kpe_b = small[:, MLA_Q_LORA + MLA_KV_LORA + LANES:]
    cos = cos_ref[...]
    sin = sin_ref[...]
    cos_h = jnp.concatenate([cos] * MLA_HEADS, axis=1)
    sin_h = jnp.concatenate([sin] * MLA_HEADS, axis=1)

    qq = _bdot(_rms_rows(c_q, qn_ref[...]), wq_ref[...])
    q_out[...] = (qq[:, :MLA_QK_W] * cos_h + qq[:, MLA_QK_W:] * sin_h).astype(BF16)

    kv = _bdot(_rms_rows(c_kv, kvn_ref[...]), wkv_ref[...])
    kpe = kpe_a * cos + kpe_b * sin
    k_out[...] = (kv[:, :MLA_QK_W] + jnp.concatenate([kpe] * MLA_HEADS, axis=1)).astype(BF16)
    v_out[...] = kv[:, MLA_QK_W:].T.astype(BF16)


def _rope_head_tables(L):
    inv = ROPE_THETA ** (-jnp.arange(0, MLA_ROPE, 2, dtype=F32) / MLA_ROPE)
    ang = jnp.arange(L, dtype=F32)[:, None] * inv[None, :]
    cos, sin = jnp.cos(ang), jnp.sin(ang)
    ones = jnp.ones((L, MLA_NOPE), F32)
    zn = jnp.zeros((L, MLA_NOPE), F32)
    zp = jnp.zeros((L, MLA_HEAD_PAD - MLA_NOPE - MLA_ROPE), F32)
    return (jnp.concatenate([ones, cos, cos, zp], axis=1), jnp.concatenate([zn, sin, sin, zp], axis=1))


def _mla_weights(w_in, w_uq, w_ukv):
    s1 = HY_U_W
    s2 = s1 + MLA_Q_LORA
    s3 = s2 + MLA_KV_LORA
    kx1, kx2 = w_in[:, s3:s3 + HALF_ROPE], w_in[:, s3 + HALF_ROPE:]
    zl = jnp.zeros((w_in.shape[0], MLA_NOPE), F32)
    zr = jnp.zeros((w_in.shape[0], MLA_HEAD_PAD - MLA_NOPE - MLA_ROPE), F32)
    kpe_a = jnp.concatenate([zl, kx1, kx2, zr], axis=1)
    kpe_b = jnp.concatenate([zl, -kx2, kx1, zr], axis=1)
    w_small = jnp.concatenate([w_in[:, s1:s3], kpe_a, kpe_b], axis=1)
    q3 = w_uq.reshape(MLA_Q_LORA, MLA_HEADS, MLA_NOPE + MLA_ROPE) * MLA_SCALE
    nope, x1, x2 = q3[..., :MLA_NOPE], q3[..., MLA_NOPE:MLA_NOPE + HALF_ROPE], q3[..., MLA_NOPE + HALF_ROPE:]
    zq = jnp.zeros((MLA_Q_LORA, MLA_HEADS, MLA_HEAD_PAD - MLA_NOPE - MLA_ROPE), F32)
    wqa = jnp.concatenate([nope, x1, x2, zq], axis=-1).reshape(MLA_Q_LORA, MLA_QK_W)
    wqb = jnp.concatenate([jnp.zeros_like(nope), -x2, x1, zq], axis=-1).reshape(MLA_Q_LORA, MLA_QK_W)
    kv3 = w_ukv.reshape(MLA_KV_LORA, MLA_HEADS, MLA_NOPE + MLA_V)
    wkn = jnp.concatenate([kv3[..., :MLA_NOPE], jnp.zeros((MLA_KV_LORA, MLA_HEADS, MLA_HEAD_PAD - MLA_NOPE), F32)],
                          axis=-1).reshape(MLA_KV_LORA, MLA_QK_W)
    wv = kv3[..., MLA_NOPE:].reshape(MLA_KV_LORA, MLA_V_W)
    return (w_in[:, :s1].astype(BF16), w_small.astype(BF16), jnp.concatenate([wqa, wqb], axis=1).astype(BF16),
            jnp.concatenate([wkn, wv], axis=1).astype(BF16))


def _in_proj(x, w_in, w_uq, w_ukv, q_norm, kv_norm, *, tm=256):
    B, L, D = x.shape
    tm = min(tm, L)
    wu, ws, wq, wkv = _mla_weights(w_in, w_uq, w_ukv)
    cos, sin = _rope_head_tables(L)
    row = lambda w: pl.BlockSpec((None, tm, w), lambda b, i: (b, i, 0))
    full = lambda a: pl.BlockSpec(a.shape, lambda b, i: (0, 0))
    pos = pl.BlockSpec((tm, LANES), lambda b, i: (i, 0))
    qn = q_norm.reshape(1, -1)
    kvn = kv_norm.reshape(1, -1)
    return pl.pallas_call(
        _in_proj_kernel,
        grid=(B, L // tm),
        in_specs=[row(D), full(wu), full(ws), full(wq), full(wkv), full(qn), full(kvn), pos, pos],
        out_specs=[row(HY_U_W), row(MLA_QK_W), row(MLA_QK_W),
                   pl.BlockSpec((None, MLA_V_W, tm), lambda b, i: (b, 0, i))],
        out_shape=[jax.ShapeDtypeStruct((B, L, HY_U_W), F32), jax.ShapeDtypeStruct((B, L, MLA_QK_W), BF16),
                   jax.ShapeDtypeStruct((B, L, MLA_QK_W), BF16), jax.ShapeDtypeStruct((B, MLA_V_W, L), BF16)],
        compiler_params=_cparams(("parallel", "parallel")),
        name="in_proj",
    )(x, wu, ws, wq, wkv, qn, kvn, cos, sin)


MLA_PAIR = 2
ATTN_SUM_ROWS = 8


def _attn_kernel(q_ref, k_ref, vt_ref, o_ref, s_ref, p_ref, mx_ref, al_ref, m_ref, l_ref, acc_ref, *, tk):
    tq = q_ref.shape[0]
    nk = k_ref.shape[0] // tk
    heads = range(MLA_PAIR)
    head_lanes = lambda h: slice(h * MLA_HEAD_PAD, (h + 1) * MLA_HEAD_PAD)
    head_rows = lambda h: slice(h * MLA_V, (h + 1) * MLA_V)

    m_ref[...] = jnp.full_like(m_ref, -jnp.inf)
    l_ref[...] = jnp.zeros_like(l_ref)
    acc_ref[...] = jnp.zeros_like(acc_ref)
    p_ref[1] = jnp.zeros_like(p_ref[1])
    al_ref[1] = jnp.ones_like(al_ref[1])

    ones = jnp.ones((ATTN_SUM_ROWS, tk), BF16)

    def scores(j, buf):
        keys = pl.ds(pl.multiple_of(j * tk, tk), tk)
        for h in heads:
            st = lax.dot_general(k_ref[keys, head_lanes(h)], q_ref[:, head_lanes(h)],
                                 (((1,), (1,)), ((), ())), preferred_element_type=F32)
            s_ref[buf, h] = st
            mx_ref[buf, h] = jnp.max(st, axis=0, keepdims=True)

    def softmax(buf):
        for h in heads:
            m_old = m_ref[h]
            m_new = jnp.maximum(m_old, mx_ref[buf, h])
            al_ref[buf, h] = jnp.exp2(m_old - m_new)
            m_ref[h] = m_new
            p_ref[buf, h] = jnp.exp2(s_ref[buf, h] - m_new).astype(BF16)

    def values(j, buf):
        keys = pl.ds(pl.multiple_of(j * tk, tk), tk)
        for h in heads:
            p = p_ref[buf, h]
            alpha = al_ref[buf, h]
            pv = jnp.dot(vt_ref[head_rows(h), keys], p, preferred_element_type=F32)
            acc_ref[head_rows(h), :] = alpha * acc_ref[head_rows(h), :] + pv
            psum = jnp.dot(ones, p, preferred_element_type=F32)
            l_ref[h] = alpha * l_ref[h] + psum[0:1, :]

    scores(0, 0)

    def pair(i, carry):
        j = 2 * i
        softmax(0)
        scores(j + 1, 1)
        values(jnp.maximum(j - 1, 0), 1)
        softmax(1)
        scores(jnp.minimum(j + 2, nk - 1), 0)
        values(j, 0)
        return carry

    lax.fori_loop(0, nk // 2, pair, 0)
    values(nk - 1, 1)

    inv = jnp.concatenate([jnp.broadcast_to(1.0 / l_ref[h], (MLA_V, tq)) for h in heads], axis=0)
    o_ref[...] = (acc_ref[...] * inv).T


ATTN_MIN_KEY_BLOCKS = 2


def _attention(q, k, vt, *, tq=512, tk=1024):
    B, L, _ = q.shape
    tq, tk = min(tq, L), min(tk, L // ATTN_MIN_KEY_BLOCKS)
    pairs = MLA_HEADS // MLA_PAIR
    return pl.pallas_call(
        functools.partial(_attn_kernel, tk=tk),
        grid=(B, pairs, L // tq),
        in_specs=[pl.BlockSpec((None, tq, MLA_PAIR * MLA_HEAD_PAD), lambda b, h, qi: (b, qi, h)),
                  pl.BlockSpec((None, L, MLA_PAIR * MLA_HEAD_PAD), lambda b, h, qi: (b, 0, h)),
                  pl.BlockSpec((None, LANES, L), lambda b, h, qi: (b, h, 0))],
        out_specs=pl.BlockSpec((None, tq, LANES), lambda b, h, qi: (b, qi, h)),
        out_shape=jax.ShapeDtypeStruct((B, L, MLA_V_W), F32),
        scratch_shapes=[pltpu.VMEM((2, MLA_PAIR, tk, tq), F32), pltpu.VMEM((2, MLA_PAIR, tk, tq), BF16),
                        pltpu.VMEM((2, MLA_PAIR, 1, tq), F32), pltpu.VMEM((2, MLA_PAIR, 1, tq), F32),
                        pltpu.VMEM((MLA_PAIR, 1, tq), F32),
                        pltpu.VMEM((MLA_PAIR, 1, tq), F32), pltpu.VMEM((LANES, tq), F32)],
        compiler_params=_cparams(("parallel", "parallel", "parallel")),
        name="attention",
    )(q, k, vt)


def _split3_rows(x):
    hi = x.astype(BF16)
    lo = (x - hi.astype(F32)).astype(BF16)
    return jnp.concatenate([hi, lo, hi], axis=0)


def _split3_cols(m):
    hi = m.astype(BF16)
    lo = (m - hi.astype(F32)).astype(BF16)
    return jnp.concatenate([hi, hi, lo], axis=-1)


def _fft_factor(n):
    n1 = 1 << ((n.bit_length() - 1 + 1) // 2)
    return n1, n // n1


def _dft_tables(L):
    N = 2 * L
    N1, N2 = _fft_factor(N)
    h = N1 // 2
    k1 = jnp.arange(N1, dtype=jnp.int32)
    ang1 = (-2.0 * math.pi / N1) * ((k1[:, None] * k1[None, :]) % N1).astype(F32)
    f1r, f1i = jnp.cos(ang1), jnp.sin(ang1)
    lead_pair = jnp.concatenate([jnp.concatenate([f1r[:, :h], -f1i[:, :h]], axis=1),
                                 jnp.concatenate([f1i[:, :h], f1r[:, :h]], axis=1)], axis=0)
    m1 = jnp.arange(h, dtype=jnp.int32)
    cols_gen, cols_first = N1 - 1 - m1, (N1 - m1) % N1
    has_lag = (m1 > 0).astype(F32)[None, :]
    filt_fwd = jnp.concatenate([f1r[:, :h], f1i[:, :h]], axis=0)
    filt_bwd = jnp.stack([jnp.concatenate([f1r[:, cols_gen], f1i[:, cols_gen]], axis=0),
                          jnp.concatenate([f1r[:, cols_first] * has_lag, f1i[:, cols_first] * has_lag], axis=0)])
    f1rt, f1it = f1r.T[:h], f1i.T[:h]
    lead_inv = jnp.concatenate([jnp.concatenate([f1rt, f1it], axis=1),
                                jnp.concatenate([-f1it, f1rt], axis=1)], axis=0) * (1.0 / N)
    n2 = jnp.arange(N2, dtype=jnp.int32)
    freq = k1[:, None, None] + N1 * n2[None, :, None]
    idx = (freq * n2[None, None, :]) % N
    ang2 = (-2.0 * math.pi / N) * idx.astype(F32)
    gr, gi = jnp.cos(ang2), jnp.sin(ang2)
    inner_fwd = jnp.concatenate([jnp.concatenate([gr, -gi], axis=2),
                                 jnp.concatenate([gi, gr], axis=2)], axis=1)
    grt, git = jnp.swapaxes(gr, 1, 2), jnp.swapaxes(gi, 1, 2)
    inner_inv = jnp.concatenate([jnp.concatenate([grt, git], axis=2),
                                 jnp.concatenate([-git, grt], axis=2)], axis=1)
    return dict(N1=N1, N2=N2, lead_pair=_split3_cols(lead_pair), filt_fwd=_split3_cols(filt_fwd),
                filt_bwd=_split3_cols(filt_bwd), lead_inv=_split3_cols(lead_inv), inner_fwd=_split3_cols(inner_fwd),
                inner_inv=_split3_cols(inner_inv))


FFT_LANE_TILE = 1024


def _dft_lead_kernel(m_ref, x_ref, o_ref):
    k = m_ref.shape[1] // 3
    x = x_ref[...].reshape(k, x_ref.shape[-1])
    o_ref[...] = jnp.dot(m_ref[...], _split3_rows(x), preferred_element_type=F32).reshape(o_ref.shape)


def _dft_lead_gate_kernel(m_ref, x_ref, z_ref, g_ref, sk_ref, o_ref):
    k = m_ref.shape[1] // 3
    x = x_ref[...].reshape(k, x_ref.shape[-1])
    conv = jnp.dot(m_ref[...], _split3_rows(x), preferred_element_type=F32).reshape(o_ref.shape)
    o_ref[...] = g_ref[...] * (conv + sk_ref[...] * z_ref[...])


def _dft_inner_conv_kernel(mf_ref, mi_ref, a_ref, kf_ref, o_ref):
    n2 = a_ref.shape[1]
    a = a_ref[...].reshape(2 * n2, a_ref.shape[-1])
    x = jnp.dot(mf_ref[...], _split3_rows(a), preferred_element_type=F32)
    xr, xi = x[:n2], x[n2:]
    kr, ki = kf_ref[0], kf_ref[1]
    y = jnp.concatenate([xr * kr - xi * ki, xr * ki + xi * kr], axis=0)
    o_ref[...] = jnp.dot(mi_ref[...], _split3_rows(y), preferred_element_type=F32).reshape(o_ref.shape)


def _dft_inner_filter_kernel(mf_ref, a_ref, s_ref, o_ref):
    n2 = a_ref.shape[1]
    a = a_ref[...].reshape(2 * n2, a_ref.shape[-1])
    x = jnp.dot(mf_ref[...], _split3_rows(a), preferred_element_type=F32) * s_ref[...]
    o_ref[...] = x.reshape(o_ref.shape)


def _dft_lead_filter_kernel(mf_ref, mb_ref, fwd_ref, bwd_ref, o_ref):
    first = jnp.where(pl.program_id(0) == 0, 1, 0)
    a = (jnp.dot(mf_ref[...], _split3_rows(fwd_ref[...]), preferred_element_type=F32)
         + jnp.dot(mb_ref[first], _split3_rows(bwd_ref[...]), preferred_element_type=F32))
    o_ref[...] = a.reshape(o_ref.shape)


def _filter_spectrum(hf, inv_norm, tabs):
    L, Cf = hf.shape[0], hf.shape[1] // 2
    N1, N2 = tabs['N1'], tabs['N2']
    h = N1 // 2
    hv = hf.reshape(h, N2 * 2 * Cf)
    a = pl.pallas_call(
        _dft_lead_filter_kernel,
        grid=(N2,),
        in_specs=[pl.BlockSpec((2 * N1, 3 * h), lambda j: (0, 0)), pl.BlockSpec((2, 2 * N1, 3 * h), lambda j: (0, 0, 0)),
                  pl.BlockSpec((h, Cf), lambda j: (0, 2 * j)),
                  pl.BlockSpec((h, Cf), lambda j: (0, 2 * lax.rem(N2 - j, N2) + 1))],
        out_specs=pl.BlockSpec((2, N1, Cf), lambda j: (0, 0, j)),
        out_shape=jax.ShapeDtypeStruct((2, N1, N2 * Cf), F32),
        compiler_params=_cparams(("arbitrary",)),
        name="hyena_filter_dft_lead",
    )(tabs['filt_fwd'], tabs['filt_bwd'], hv, hv)
    return pl.pallas_call(
        _dft_inner_filter_kernel,
        grid=(N1,),
        in_specs=[pl.BlockSpec((None, 2 * N2, 6 * N2), lambda k1: (k1, 0, 0)),
                  pl.BlockSpec((2, None, N2, Cf), lambda k1: (0, k1, 0, 0)),
                  pl.BlockSpec((1, Cf), lambda k1: (0, 0))],
        out_specs=pl.BlockSpec((2, None, N2, Cf), lambda k1: (0, k1, 0, 0)),
        out_shape=jax.ShapeDtypeStruct((2, N1, N2, Cf), F32),
        compiler_params=_cparams(("parallel",)),
        name="hyena_filter_dft_inner",
    )(tabs['inner_fwd'], a.reshape(2, N1, N2, Cf), inv_norm)


def _long_conv_gate(z, gate, skip, kspec, order, tabs):
    B, L, C = z.shape
    N1, N2 = tabs['N1'], tabs['N2']
    h = N1 // 2
    P = B // 2
    W = N2 * C
    wc = min(FFT_LANE_TILE, W)
    zv = z.reshape(B, h, W)
    a = pl.pallas_call(
        _dft_lead_kernel,
        grid=(P, W // wc),
        in_specs=[pl.BlockSpec((2 * N1, 3 * N1), lambda p, j: (0, 0)),
                  pl.BlockSpec((2, h, wc), lambda p, j: (p, 0, j))],
        out_specs=pl.BlockSpec((None, 2, N1, wc), lambda p, j: (p, 0, 0, j)),
        out_shape=jax.ShapeDtypeStruct((P, 2, N1, W), F32),
        compiler_params=_cparams(("parallel", "parallel")),
        name="hyena_dft_lead",
    )(tabs['lead_pair'], zv)
    inner = pl.BlockSpec((None, 2, None, N2, C), lambda p, k1: (p, 0, k1, 0, 0))
    mat = pl.BlockSpec((None, 2 * N2, 6 * N2), lambda p, k1: (k1, 0, 0))
    bm = pl.pallas_call(
        _dft_inner_conv_kernel,
        grid=(P, N1),
        in_specs=[mat, mat, inner, pl.BlockSpec((2, None, N2, C), lambda p, k1: (0, k1, 0, order))],
        out_specs=inner,
        out_shape=jax.ShapeDtypeStruct((P, 2, N1, N2, C), F32),
        compiler_params=_cparams(("parallel", "parallel")),
        name="hyena_dft_inner",
    )(tabs['inner_fwd'], tabs['inner_inv'], a.reshape(P, 2, N1, N2, C), kspec)
    pair = pl.BlockSpec((2, h, wc), lambda p, j: (p, 0, j))
    sk = jnp.tile(skip.reshape(1, C), (1, wc // C))
    out = pl.pallas_call(
        _dft_lead_gate_kernel,
        grid=(P, W // wc),
        in_specs=[pl.BlockSpec((N1, 6 * N1), lambda p, j: (0, 0)),
                  pl.BlockSpec((None, 2, N1, wc), lambda p, j: (p, 0, 0, j)), pair, pair,
                  pl.BlockSpec((1, wc), lambda p, j: (0, 0))],
        out_specs=pair,
        out_shape=jax.ShapeDtypeStruct((B, h, W), F32),
        compiler_params=_cparams(("parallel", "parallel")),
        name="hyena_dft_lead_inv",
    )(tabs['lead_inv'], bm.reshape(P, 2, N1, W), zv, gate.reshape(B, h, W), sk)
    return out.reshape(B, L, C)


def _short_conv_kernel(u_ref, up_ref, un_ref, cw_ref, cb_ref, z_out, x1_out, x2_out):
    i = pl.program_id(1)
    u = u_ref[...]
    u_prev, u_next = _shifted(u, up_ref, un_ref, i, pl.num_programs(1))
    cw = cw_ref[...]
    y = u_prev * cw[0:1, :] + u * cw[1:2, :] + u_next * cw[2:3, :] + cb_ref[...]
    z_out[...] = y[:, :HY_W]
    x1_out[...] = y[:, HY_W:2 * HY_W]
    x2_out[...] = y[:, 2 * HY_W:]


def _short_conv(u, conv_w, conv_b, *, tm=512):
    B, L, W = u.shape
    tm = min(tm, L)
    main, prev, nxt = _halo_specs(tm, W, L, lead=0)
    part = pl.BlockSpec((None, tm, HY_W), lambda b, i: (b, i, 0))
    return pl.pallas_call(
        _short_conv_kernel,
        grid=(B, L // tm),
        in_specs=[main, prev, nxt, pl.BlockSpec((3, W), lambda b, i: (0, 0)), pl.BlockSpec((1, W), lambda b, i: (0, 0))],
        out_specs=[part] * 3,
        out_shape=[jax.ShapeDtypeStruct((B, L, HY_W), F32)] * 3,
        compiler_params=_cparams(("parallel", "parallel")),
        name="hyena_short_conv",
    )(u, u, u, conv_w, conv_b.reshape(1, W))


HY_FILT_HID = 64
HY_FILT_OUT = 2 * HY_ORDER * HY_W
TWO_PI = 2.0 * math.pi


def _hdot(a, b):
    return jnp.dot(a, b, precision=lax.Precision.HIGHEST, preferred_element_type=F32)


def _filter_mlp_kernel(bands_ref, w1t_ref, w1s_ref, w1c_ref, b1_ref, w2_ref, b2_ref, w3_ref, fr_ref, dec_ref,
                       h_out, abs_out, *, seq_len):
    i = pl.program_id(0)
    tm = h_out.shape[0]
    row = i * tm + lax.broadcasted_iota(jnp.int32, (tm, 1), 0)
    t = row.astype(F32) / seq_len
    ph = (TWO_PI * t) * bands_ref[...]
    fr = fr_ref[...]
    pre1 = t * w1t_ref[...] + _hdot(jnp.sin(ph), w1s_ref[...]) + _hdot(jnp.cos(ph), w1c_ref[...]) + b1_ref[...]
    h1 = jnp.sin(fr * pre1)
    h2 = jnp.sin(fr * (_hdot(h1, w2_ref[...]) + b2_ref[...]))
    hf = _hdot(h2, w3_ref[...]) * jnp.exp(-t * jnp.abs(dec_ref[...]))
    h_out[...] = hf

    @pl.when(i == 0)
    def _():
        abs_out[...] = jnp.zeros_like(abs_out)

    col = lax.broadcasted_iota(jnp.int32, (1, HY_FILT_OUT), 1)
    used = jnp.where(jnp.logical_and(row == 0, col >= HY_FILT_OUT // 2), 0.0, jnp.abs(hf))
    abs_out[...] += jnp.sum(used, axis=0, keepdims=True)


def _hyena_filter_spectrum(L, w1, b1, w2, b2, w3, freq, decay, tabs, *, tm=256):
    tm = min(tm, L)
    bands = _pad_axis(jnp.linspace(1e-4, HY_BANDS - 1, HY_BANDS, dtype=F32).reshape(1, HY_BANDS), 1, LANES)
    w1p = _pad_axis(w1, 1, LANES)
    w1t = w1p[0:1]
    w1s = _pad_axis(w1p[1:1 + HY_BANDS], 0, LANES)
    w1c = _pad_axis(w1p[1 + HY_BANDS:], 0, LANES)
    pad_vec = lambda v: _pad_axis(v.reshape(1, -1), 1, LANES)
    w2p = _pad_axis(_pad_axis(w2, 0, LANES), 1, LANES)
    w3p = _pad_axis(w3, 0, LANES)
    full = lambda a: pl.BlockSpec(a.shape, lambda i: (0, 0))
    args = (bands, w1t, w1s, w1c, pad_vec(b1), w2p, pad_vec(b2), w3p, pad_vec(freq), decay.reshape(1, HY_FILT_OUT))
    hf, abs_sum = pl.pallas_call(
        functools.partial(_filter_mlp_kernel, seq_len=L),
        grid=(L // tm,),
        in_specs=[full(a) for a in args],
        out_specs=[pl.BlockSpec((tm, HY_FILT_OUT), lambda i: (i, 0)), pl.BlockSpec((1, HY_FILT_OUT), lambda i: (0, 0))],
        out_shape=[jax.ShapeDtypeStruct((L, HY_FILT_OUT), F32), jax.ShapeDtypeStruct((1, HY_FILT_OUT), F32)],
        compiler_params=_cparams(("arbitrary",)),
        name="hyena_filter_mlp",
    )(*args)
    half = HY_FILT_OUT // 2
    inv_norm = 1.0 / (abs_sum[:, :half] + abs_sum[:, half:] + 1e-6)
    return _filter_spectrum(hf, inv_norm, tabs)


def _hyena(u, conv_w, conv_b, skip, kspec, tabs):
    z, x1, x2 = _short_conv(u, conv_w, conv_b)
    z = _long_conv_gate(z, x1, skip[0], kspec, 0, tabs)
    return _long_conv_gate(z, x2, skip[1], kspec, 1, tabs)


def _mix_out_ln_kernel(a_ref, b_ref, wa_ref, wb_ref, res_ref, g_ref, bb_ref, out_ref):
    m = (jnp.dot(a_ref[...].astype(BF16), wa_ref[...], preferred_element_type=F32)
         + jnp.dot(b_ref[...].astype(BF16), wb_ref[...], preferred_element_type=F32))
    out_ref[...] = _layer_norm_rows(DN_ALPHA * res_ref[...] + m, g_ref[...], bb_ref[...])


def _mix_out_ln(y_hy, y_mla, w_out, res, ln_g, ln_b, *, tm=512):
    T, D = res.shape
    tm = min(tm, T)
    wa = w_out[:HY_W].astype(BF16)
    wb = w_out[HY_W:].astype(BF16)
    row = lambda w: pl.BlockSpec((tm, w), lambda i: (i, 0))
    full = lambda a: pl.BlockSpec(a.shape, lambda i: (0, 0))
    vec = pl.BlockSpec((1, D), lambda i: (0, 0))
    return pl.pallas_call(
        _mix_out_ln_kernel,
        grid=(T // tm,),
        in_specs=[row(HY_W), row(MLA_V_W), full(wa), full(wb), row(D), vec, vec],
        out_specs=row(D),
        out_shape=jax.ShapeDtypeStruct((T, D), F32),
        compiler_params=_cparams(("parallel",)),
        name="mix_out_ln",
    )(y_hy, y_mla, wa, wb, res, ln_g.reshape(1, D), ln_b.reshape(1, D))


def _hyena_mla_layer(x, p, kspec, tabs, ln_g, ln_b):
    B, L, D = x.shape
    u, q, k, v = _in_proj(x, p['w_in'], p['w_uq'], p['w_ukv'], p['q_norm'], p['kv_norm'])
    y_hy = _hyena(u, p['conv_w'], p['conv_b'], p['skip'], kspec, tabs)
    y_mla = _attention(q, k, v)
    return _mix_out_ln(y_hy.reshape(B * L, HY_W), y_mla.reshape(B * L, MLA_V_W), p['w_out'],
                       x.reshape(B * L, D), ln_g, ln_b).reshape(B, L, D)


MOE_ROWS = 256
ROUTE_W = LANES
NEG_BIG = -1e30


def _router_kernel(x_ref, w_ref, idx_out, gate_out, rank_out, cnt_out, cnt_ref):
    logits = _hdot(x_ref[...], w_ref[...])
    lane = lax.broadcasted_iota(jnp.int32, logits.shape, 1)
    is_group = lane < MOE_GROUPS
    gl = jnp.where(is_group, logits, NEG_BIG)
    gmax = jnp.max(gl, axis=-1, keepdims=True)
    g_top = jnp.min(jnp.where(gl == gmax, lane, ROUTE_W), axis=-1, keepdims=True)
    g_prob = 1.0 / jnp.sum(jnp.where(is_group, jnp.exp(gl - gmax), 0.0), axis=-1, keepdims=True)

    first = MOE_GROUPS + g_top * MOE_PER_GROUP
    in_grp = jnp.logical_and(lane >= first, lane < first + MOE_PER_GROUP)
    el = jnp.where(in_grp, logits, NEG_BIG)
    emax = jnp.max(el, axis=-1, keepdims=True)
    ex = jnp.where(in_grp, jnp.exp(el - emax), 0.0)
    prob = ex / jnp.sum(ex, axis=-1, keepdims=True)
    pm = jnp.where(in_grp, prob, -1.0)
    p1 = jnp.max(pm, axis=-1, keepdims=True)
    l1 = jnp.min(jnp.where(pm == p1, lane, ROUTE_W), axis=-1, keepdims=True)
    pm2 = jnp.where(lane == l1, -1.0, pm)
    p2 = jnp.max(pm2, axis=-1, keepdims=True)
    l2 = jnp.min(jnp.where(pm2 == p2, lane, ROUTE_W), axis=-1, keepdims=True)
    tot = p1 + p2
    e1 = l1 - MOE_GROUPS
    e2 = l2 - MOE_GROUPS
    idx_out[...] = jnp.where(lane == 0, e1, jnp.where(lane == 1, e2, 0))
    gate_out[...] = jnp.where(lane == 0, g_prob * (p1 / tot), jnp.where(lane == 1, g_prob * (p2 / tot), 0.0))

    @pl.when(pl.program_id(0) == 0)
    def _():
        cnt_ref[...] = jnp.zeros_like(cnt_ref)

    tm = logits.shape[0]
    oh1 = jnp.where(lane == l1, 1.0, 0.0)
    oh2 = jnp.where(lane == l2, 1.0, 0.0)
    both = oh1 + oh2
    ri = lax.broadcasted_iota(jnp.int32, (tm, tm), 0)
    ci = lax.broadcasted_iota(jnp.int32, (tm, tm), 1)
    earlier = jnp.where(ci < ri, 1.0, 0.0).astype(BF16)
    before = cnt_ref[...] + jnp.dot(earlier, both.astype(BF16), preferred_element_type=F32)
    r1 = jnp.sum(oh1 * before, axis=-1, keepdims=True)
    r2 = jnp.sum(oh2 * before, axis=-1, keepdims=True)
    rank_out[...] = jnp.where(lane == 0, r1, jnp.where(lane == 1, r2, 0.0)).astype(jnp.int32)
    cnt_ref[...] += jnp.sum(both, axis=0, keepdims=True)
    cnt_out[...] = cnt_ref[...]


def _router(xt, w_group, w_expert, *, tm=512):
    T, D = xt.shape
    tm = min(tm, T)
    w = _pad_axis(jnp.concatenate([w_group, w_expert], axis=1), 1, ROUTE_W)
    row = pl.BlockSpec((tm, ROUTE_W), lambda i: (i, 0))
    one = pl.BlockSpec((1, ROUTE_W), lambda i: (0, 0))
    return pl.pallas_call(
        _router_kernel,
        grid=(T // tm,),
        in_specs=[pl.BlockSpec((tm, D), lambda i: (i, 0)), pl.BlockSpec((D, ROUTE_W), lambda i: (0, 0))],
        out_specs=[row, row, row, one],
        out_shape=[jax.ShapeDtypeStruct((T, ROUTE_W), jnp.int32), jax.ShapeDtypeStruct((T, ROUTE_W), F32),
                   jax.ShapeDtypeStruct((T, ROUTE_W), jnp.int32), jax.ShapeDtypeStruct((1, ROUTE_W), F32)],
        scratch_shapes=[pltpu.VMEM((1, ROUTE_W), F32)],
        compiler_params=_cparams(("arbitrary",)),
        name="moe_router",
    )(xt, w)


MOE_TILE = 256


def _tile_major(dest, tm):
    nt = dest.shape[0] // tm
    return dest.reshape(nt, tm, MOE_TOPK).transpose(0, 2, 1).reshape(nt, 1, MOE_TOPK * tm)


def _dispatch_kernel(dest_ref, x_ref, rows_in, rows_out, sem):
    del rows_in
    tm = x_ref.shape[0]

    def row_copy(j):
        r = lax.rem(j, tm)
        return pltpu.make_async_copy(x_ref.at[pl.ds(r, 1)], rows_out.at[pl.ds(dest_ref[0, j], 1)], sem)

    def start(j, c):
        row_copy(j).start()
        return c

    def wait(j, c):
        row_copy(j).wait()
        return c

    lax.fori_loop(0, MOE_TOPK * tm, start, 0, unroll=8)
    lax.fori_loop(0, MOE_TOPK * tm, wait, 0, unroll=8)


def _dispatch(xt, dest, n_rows):
    T, D = xt.shape
    tm = min(MOE_TILE, T)
    return pl.pallas_call(
        _dispatch_kernel,
        grid=(T // tm,),
        in_specs=[pl.BlockSpec((None, 1, MOE_TOPK * tm), lambda i: (i, 0, 0), memory_space=pltpu.SMEM),
                  pl.BlockSpec((tm, D), lambda i: (i, 0)), pl.BlockSpec(memory_space=pl.ANY)],
        out_specs=pl.BlockSpec(memory_space=pl.ANY),
        out_shape=jax.ShapeDtypeStruct((n_rows, D), F32),
        scratch_shapes=[pltpu.SemaphoreType.DMA(())],
        input_output_aliases={2: 0},
        compiler_params=_cparams(("arbitrary",)),
        name="moe_dispatch",
    )(_tile_major(dest, tm), xt, jnp.zeros((n_rows, D), F32))


def _expert_kernel(blk_e_ref, n_used_ref, x_ref, w1_ref, w3_ref, w2_ref, o_ref):
    i = pl.program_id(0)

    @pl.when(i < n_used_ref[0])
    def _():
        xb = x_ref[...].astype(BF16)
        h1 = jnp.dot(xb, w1_ref[...], preferred_element_type=F32)
        h3 = jnp.dot(xb, w3_ref[...], preferred_element_type=F32)
        h = (h1 * jax.nn.sigmoid(h1) * h3).astype(BF16)
        o_ref[...] = jnp.dot(h, w2_ref[...], preferred_element_type=F32)

    @pl.when(i >= n_used_ref[0])
    def _():
        o_ref[...] = jnp.zeros_like(o_ref)


def _experts(rows, blk_e, n_used, w1, w3, w2):
    P, D = rows.shape
    n_blocks = blk_e.shape[0]
    blk = lambda i, be, nu: (jnp.minimum(i, nu[0] - 1), 0)
    wspec = lambda shape: pl.BlockSpec((None,) + shape, lambda i, be, nu: (be[jnp.minimum(i, nu[0] - 1)], 0, 0))
    return pl.pallas_call(
        _expert_kernel,
        grid_spec=pltpu.PrefetchScalarGridSpec(
            num_scalar_prefetch=2,
            grid=(n_blocks,),
            in_specs=[pl.BlockSpec((MOE_ROWS, D), blk), wspec((D, MOE_FF)), wspec((D, MOE_FF)), wspec((MOE_FF, D))],
            out_specs=pl.BlockSpec((MOE_ROWS, D), lambda i, be, nu: (i, 0)),
        ),
        out_shape=jax.ShapeDtypeStruct((P, D), F32),
        compiler_params=_cparams(("arbitrary",)),
        name="moe_experts",
    )(blk_e, n_used, rows, w1, w3, w2)


def _combine_ln_kernel(dest_ref, dest_next_ref, eo_hbm, gate_ref, x_ref, g_ref, b_ref, out_ref, buf, sems):
    i = pl.program_id(0)
    tm = x_ref.shape[0]
    slot = lax.rem(i, 2)

    def row_copy(idx_ref, j, s):
        k = j // tm
        r = j - k * tm
        return pltpu.make_async_copy(eo_hbm.at[pl.ds(idx_ref[0, j], 1)], buf.at[s, k, pl.ds(r, 1)], sems.at[s])

    def start_all(idx_ref, s):
        def body(j, c):
            row_copy(idx_ref, j, s).start()
            return c
        lax.fori_loop(0, MOE_TOPK * tm, body, 0, unroll=8)

    @pl.when(i == 0)
    def _():
        start_all(dest_ref, slot)

    @pl.when(i + 1 < pl.num_programs(0))
    def _():
        start_all(dest_next_ref, 1 - slot)

    def wait(j, c):
        row_copy(dest_ref, j, slot).wait()
        return c

    lax.fori_loop(0, MOE_TOPK * tm, wait, 0, unroll=8)
    gate = gate_ref[...]
    f = gate[:, 0:1] * buf[slot, 0] + gate[:, 1:2] * buf[slot, 1]
    out_ref[...] = _layer_norm_rows(DN_ALPHA * x_ref[...] + f, g_ref[...], b_ref[...])


def _combine_ln(eo, dest, gate, xt, ln_g, ln_b):
    T, D = xt.shape
    tm = min(MOE_TILE, T)
    nt = T // tm
    dest_t = _tile_major(dest, tm)
    row = pl.BlockSpec((tm, D), lambda i: (i, 0))
    vec = pl.BlockSpec((1, D), lambda i: (0, 0))
    idx = lambda off: pl.BlockSpec((None, 1, MOE_TOPK * tm), lambda i: (jnp.minimum(i + off, nt - 1), 0, 0),
                                   memory_space=pltpu.SMEM)
    return pl.pallas_call(
        _combine_ln_kernel,
        grid=(nt,),
        in_specs=[idx(0), idx(1), pl.BlockSpec(memory_space=pl.ANY), pl.BlockSpec((tm, ROUTE_W), lambda i: (i, 0)),
                  row, vec, vec],
        out_specs=row,
        out_shape=jax.ShapeDtypeStruct((T, D), F32),
        scratch_shapes=[pltpu.VMEM((2, MOE_TOPK, tm, D), F32), pltpu.SemaphoreType.DMA((2,))],
        compiler_params=_cparams(("arbitrary",)),
        name="moe_combine_ln",
    )(dest_t, dest_t, eo, gate, xt, ln_g.reshape(1, D), ln_b.reshape(1, D))


def _dispatch_plan(expert, rank, counts, n_blocks):
    cnt = counts[0, MOE_GROUPS:MOE_GROUPS + MOE_EXPERTS].astype(jnp.int32)
    padded = (cnt + MOE_ROWS - 1) // MOE_ROWS * MOE_ROWS
    pad_end = jnp.cumsum(padded)
    pad_start = pad_end - padded
    dest = jnp.take(pad_start, expert, axis=0) + rank
    first_row = jnp.arange(n_blocks, dtype=jnp.int32) * MOE_ROWS
    blk_e = jnp.minimum(jnp.sum(first_row[:, None] >= pad_end[None, :], axis=1), MOE_EXPERTS - 1).astype(jnp.int32)
    n_used = (pad_end[-1] // MOE_ROWS).astype(jnp.int32).reshape(1)
    return dest.astype(jnp.int32), blk_e, n_used


def _moe_layer(x, w_group, w_expert, w1, w3, w2, ln_g, ln_b):
    B, L, D = x.shape
    xt = x.reshape(B * L, D)
    T = B * L
    n_blocks = T * MOE_TOPK // MOE_ROWS + MOE_EXPERTS
    idx, gate, rank, counts = _router(xt, w_group, w_expert)
    dest, blk_e, n_used = _dispatch_plan(idx[:, :MOE_TOPK], rank[:, :MOE_TOPK], counts, n_blocks)
    rows = _dispatch(xt, dest, n_blocks * MOE_ROWS)
    eo = _experts(rows, blk_e, n_used, w1, w3, w2)
    return _combine_ln(eo, dest, gate, xt, ln_g, ln_b).reshape(B, L, D)


def _trunk(x, p, tabs, kspecs):
    for layer in range(DEPTH):
        i = layer // 2
        if layer % 2 == 0:
            lp = {n: p[n][i] for n in ('w_in', 'w_uq', 'w_ukv', 'q_norm', 'kv_norm', 'conv_w', 'conv_b', 'skip', 'w_out')}
            x = _hyena_mla_layer(x, lp, kspecs[i], tabs, p['ln1_g'][layer], p['ln1_b'][layer])
        else:
            lp = {n[3:]: p[n][i] for n in p if n.startswith('rw_')}
            x = _rwkv_layer(x, lp, p['ln1_g'][layer], p['ln1_b'][layer])
        x = _moe_layer(x, p['moe_w_group'][layer], p['moe_w_expert'][layer], p['moe_w1'][layer],
                       p['moe_w3'][layer], p['moe_w2'][layer], p['ln2_g'][layer], p['ln2_b'][layer])
    return x


def kernel(x_prompt, x_sample, ln1_g, ln1_b, ln2_g, ln2_b, mix_w_in, hy_conv_w, hy_conv_b, hy_ffn_w1,
           hy_ffn_b1, hy_ffn_w2, hy_ffn_b2, hy_ffn_w3, hy_sin_freq, hy_decay, hy_skip, mla_q_norm, mla_w_uq,
           mla_kv_norm, mla_w_ukv, mix_w_out, rw_mu, rw_w_r, rw_w_k, rw_w_v, rw_w0, rw_w1, rw_w2, rw_a0,
           rw_a1, rw_a2, rw_g1, rw_g2, rw_k_k, rw_k_a, rw_r_k, rw_gn_g, rw_gn_b, rw_w_o, moe_w_group,
           moe_w_expert, moe_w1, moe_w3, moe_w2):
    p = dict(ln1_g=ln1_g, ln1_b=ln1_b, ln2_g=ln2_g, ln2_b=ln2_b, w_in=mix_w_in, conv_w=hy_conv_w,
             conv_b=hy_conv_b, skip=hy_skip, q_norm=mla_q_norm, w_uq=mla_w_uq, kv_norm=mla_kv_norm,
             w_ukv=mla_w_ukv, w_out=mix_w_out, rw_mu=rw_mu, rw_w_r=rw_w_r, rw_w_k=rw_w_k, rw_w_v=rw_w_v,
             rw_w0=rw_w0, rw_w1=rw_w1, rw_w2=rw_w2, rw_a0=rw_a0, rw_a1=rw_a1, rw_a2=rw_a2, rw_g1=rw_g1,
             rw_g2=rw_g2, rw_k_k=rw_k_k, rw_k_a=rw_k_a, rw_r_k=rw_r_k, rw_gn_g=rw_gn_g, rw_gn_b=rw_gn_b,
             rw_w_o=rw_w_o, moe_w_group=moe_w_group, moe_w_expert=moe_w_expert,
             moe_w1=moe_w1.astype(BF16), moe_w3=moe_w3.astype(BF16), moe_w2=moe_w2.astype(BF16))
    outs = []
    for x in (x_prompt, x_sample):
        L = x.shape[1]
        tabs = _dft_tables(L)
        kspecs = [_hyena_filter_spectrum(L, hy_ffn_w1[i], hy_ffn_b1[i], hy_ffn_w2[i], hy_ffn_b2[i],
                                         hy_ffn_w3[i], hy_sin_freq[i], hy_decay[i], tabs)
                  for i in range((DEPTH + 1) // 2)]
        outs.append(_trunk(x, p, tabs, kspecs))
    return tuple(outs)
```

```python
import functools
import math

import jax
import jax.numpy as jnp
import numpy as np
from jax import lax
from jax.experimental import pallas as pl
from jax.experimental.pallas import tpu as pltpu

F32 = jnp.float32
BF16 = jnp.bfloat16

LANES = 128
VMEM_LIMIT_BYTES = 56 * 1024 * 1024

D_MODEL = 1024
DEPTH = 4
HY_W = D_MODEL // 2
HY_ORDER = 2
HY_BANDS = 16
MLA_HEADS = 8
MLA_NOPE = 64
MLA_ROPE = 32
MLA_V = 64
MLA_Q_LORA = 256
MLA_KV_LORA = 128
ROPE_THETA = 10000.0
RW_HEAD = 64
RW_HEADS = D_MODEL // RW_HEAD
RW_GN_EPS = 64e-5
MOE_GROUPS = 4
MOE_PER_GROUP = 8
MOE_EXPERTS = MOE_GROUPS * MOE_PER_GROUP
MOE_TOPK = 2
MOE_FF = 512
LN_EPS = 1e-5
RMS_EPS = 1e-6
DN_ALPHA = (2 * DEPTH) ** 0.25

RW_CHUNK = 64
HEADS_PER_SLAB = LANES // RW_HEAD


def _cparams(sem):
    return pltpu.CompilerParams(dimension_semantics=sem, vmem_limit_bytes=VMEM_LIMIT_BYTES)


def _run_staged(chains, skew):
    done = [False] * len(chains)
    rnd = 0
    while not all(done):
        for c, ch in enumerate(chains):
            if rnd >= skew * c and not done[c]:
                done[c] = not next(ch, False)
        rnd += 1


def _bdot(a, b):
    return jnp.dot(a.astype(BF16), b.astype(BF16), preferred_element_type=F32)


def _bdot_nt(a, b):
    return lax.dot_general(a.astype(BF16), b.astype(BF16), (((1,), (1,)), ((), ())),
                           preferred_element_type=F32)


def _bdot_tn(a, b):
    return lax.dot_general(a.astype(BF16), b.astype(BF16), (((0,), (0,)), ((), ())),
                           preferred_element_type=F32)


def _rwkv_scan_kernel(r_ref, lw_ref, k_ref, v_ref, kk_ref, b_ref, g_ref, rk_ref, gng_ref, gnb_ref,
                      o_ref, s_ref, *, tb):
    C = RW_CHUNK
    z = pl.program_id(0)
    fwd = z == 0

    @pl.when(pl.program_id(3) == 0)
    def _():
        s_ref[...] = jnp.zeros_like(s_ref)

    lane = lax.broadcasted_iota(jnp.int32, (1, LANES), 1)
    m0 = lane < RW_HEAD
    sgn = jnp.where(fwd, 1, -1)
    ri = lax.broadcasted_iota(jnp.int32, (C, C), 0)
    ci = lax.broadcasted_iota(jnp.int32, (C, C), 1)
    tri_c = jnp.where((ci - ri) * sgn <= 0, 1.0, 0.0).astype(BF16)
    r2 = lax.broadcasted_iota(jnp.int32, (2 * C, 2 * C), 0)
    c2 = lax.broadcasted_iota(jnp.int32, (2 * C, 2 * C), 1)
    head_r = jnp.where(r2 >= C, 1, 0)
    head_c = jnp.where(c2 >= C, 1, 0)
    lag = jnp.where(head_r == head_c, (c2 - r2) * sgn, 1)
    strict = lag < 0
    incl = lag <= 0

    def stack(x):
        return jnp.concatenate([jnp.where(m0, x, 0.0), jnp.where(m0, 0.0, x)], axis=0)

    def head_sum(x):
        s0 = jnp.sum(jnp.where(m0, x, 0.0), axis=-1, keepdims=True)
        s1 = jnp.sum(jnp.where(m0, 0.0, x), axis=-1, keepdims=True)
        return jnp.where(m0, s0, s1)

    def chunk(j, carry):
        start = jnp.where(fwd, j * C, tb - (j + 1) * C)
        start = pl.multiple_of(start, C)
        sl = pl.ds(start, C)
        _run_staged([slab_chunk(sl, slab) for slab in range(s_ref.shape[0])], skew=0)
        return carry

    def slab_chunk(sl, slab):
        ln = slice(slab * LANES, (slab + 1) * LANES)
        rk = rk_ref[:, ln]
        gng = gng_ref[:, ln]
        gnb = gnb_ref[:, ln]
        r = r_ref[sl, ln]
        lw = lw_ref[sl, ln]
        k = k_ref[sl, ln]
        v = v_ref[sl, ln]
        kk = kk_ref[sl, ln]
        b = b_ref[sl, ln]
        g = g_ref[sl, ln]
        S = s_ref[slab]

        lw_hi = lw.astype(BF16)
        rem = lw - lw_hi.astype(F32)
        lw_mid = rem.astype(BF16)
        lw_lo = (rem - lw_mid.astype(F32)).astype(BF16)
        cl3 = jnp.dot(tri_c, jnp.concatenate([lw_hi, lw_mid, lw_lo], axis=1), preferred_element_type=F32)
        cl = cl3[:, :LANES] + cl3[:, LANES:2 * LANES] + cl3[:, 2 * LANES:]
        yield True
        cl_last = jnp.where(fwd, cl[C - 1:C, :], cl[0:1, :])
        eneg = jnp.exp(-cl)
        epos = jnp.exp(cl)
        ehat = jnp.exp(cl_last - cl)
        at = -kk * jnp.exp(cl - lw)
        la_lr = jnp.concatenate([stack(at), stack(r * epos)], axis=0).astype(BF16)
        rb_rk = jnp.concatenate([stack(b * eneg), stack(k * eneg)], axis=0).astype(BF16)
        v2 = stack(v)

        sc = _bdot_nt(la_lr, rb_rk)
        hs = _bdot_nt(la_lr, S)
        ys = hs[2 * C:]
        yield True
        mab = jnp.where(strict, sc[:2 * C, :2 * C], 0.0)
        mak = jnp.where(strict, sc[:2 * C, 2 * C:], 0.0)
        nrb = jnp.where(incl, sc[2 * C:, :2 * C], 0.0)
        nrk = jnp.where(incl, sc[2 * C:, 2 * C:], 0.0)
        x = hs[:2 * C] + _bdot(mak, v2)
        mp = mab
        yield True
        x = x + _bdot(mp, x)
        for _ in range(int(math.log2(C)) - 1):
            mp = _bdot(mp, mp)
            yield True
            x = x + _bdot(mp, x)
        yield True
        uv = jnp.concatenate([x, v2], axis=0)
        y2 = ys + _bdot(jnp.concatenate([nrb, nrk], axis=1), uv)
        bh_kh = jnp.concatenate([stack(b * ehat), stack(k * ehat)], axis=0)
        s_ref[slab] = S * jnp.exp(cl_last) + _bdot_tn(uv, bh_kh)
        yield True
        y = y2[:C] + y2[C:]

        mean = head_sum(y) * (1.0 / RW_HEAD)
        d = y - mean
        var = head_sum(d * d) * (1.0 / RW_HEAD)
        yn = d * lax.rsqrt(var + RW_GN_EPS) * gng + gnb
        bonus = head_sum(r * k * rk) * v
        o_ref[sl, ln] = (yn + bonus) * g

    lax.fori_loop(0, tb // C, chunk, 0)


RW_SLABS_PER_STEP = 8


def _rwkv_scan(r, lw, k, v, kk, b, g, r_k, gn_g, gn_b, *, tb=512):
    Z, B, L, D = r.shape
    tb = min(tb, L)
    nblk = L // tb
    ns = min(RW_SLABS_PER_STEP, D // LANES)
    width = ns * LANES

    def amap(z, bb, h, i):
        return (z, bb, jnp.where(z == 0, i, nblk - 1 - i), h)

    act = pl.BlockSpec((None, None, tb, width), amap)
    par = pl.BlockSpec((1, width), lambda z, bb, h, i: (0, h))
    return pl.pallas_call(
        functools.partial(_rwkv_scan_kernel, tb=tb),
        grid=(Z, B, D // width, nblk),
        in_specs=[act] * 7 + [par] * 3,
        out_specs=act,
        out_shape=jax.ShapeDtypeStruct((Z, B, L, D), F32),
        scratch_shapes=[pltpu.VMEM((ns, LANES, LANES), F32)],
        compiler_params=_cparams(("parallel", "parallel", "parallel", "arbitrary")),
        name="rwkv_scan",
    )(r, lw, k, v, kk, b, g, r_k.reshape(1, D), gn_g.reshape(1, D), gn_b.reshape(1, D))


HALO_ROWS = 8


def _halo_specs(tm, width, seq_len, lead):
    per = tm // HALO_ROWS
    last = seq_len // HALO_ROWS - 1

    def main(*g):
        return (g[lead], g[lead + 1], 0)

    def prev(*g):
        return (g[lead], jnp.maximum(g[lead + 1] * per - 1, 0), 0)

    def nxt(*g):
        return (g[lead], jnp.minimum((g[lead + 1] + 1) * per, last), 0)

    return (pl.BlockSpec((None, tm, width), main), pl.BlockSpec((None, HALO_ROWS, width), prev),
            pl.BlockSpec((None, HALO_ROWS, width), nxt))


def _shifted(x, prev_ref, next_ref, i, n_tiles):
    tm = x.shape[0]
    prev_row = jnp.where(i == 0, 0.0, prev_ref[HALO_ROWS - 1:HALO_ROWS, :])
    next_row = jnp.where(i == n_tiles - 1, 0.0, next_ref[0:1, :])
    row = lax.broadcasted_iota(jnp.int32, (tm, 1), 0)
    x_prev = jnp.where(row == 0, prev_row, pltpu.roll(x, 1, axis=0))
    x_next = jnp.where(row == tm - 1, next_row, pltpu.roll(x, tm - 1, axis=0))
    return x_prev, x_next


def _layer_norm_rows(y, g, b):
    mu = jnp.mean(y, axis=-1, keepdims=True)
    d = y - mu
    var = jnp.mean(d * d, axis=-1, keepdims=True)
    return d * lax.rsqrt(var + LN_EPS) * g + b


RW_LORA_PAD = 128
RW_GATE_PAD = 256
RW_DECAY_SCALE = math.exp(-0.5)


def _rwkv_proj_kernel(x_ref, xp_ref, xn_ref, mu_ref, wr_ref, wk_ref, wv_ref, w0_ref, w1_ref, w2_ref,
                      a0_ref, a1_ref, a2_ref, g1_ref, g2_ref, kk_ref, ka_ref,
                      r_out, lw_out, k_out, v_out, kk_out, b_out, g_out):
    z = pl.program_id(0)
    i = pl.program_id(2)
    x = x_ref[...]
    x_prev, x_next = _shifted(x, xp_ref, xn_ref, i, pl.num_programs(2))
    xx = jnp.where(z == 0, x_prev, x_next) - x
    mu = mu_ref[...]

    def mix(j):
        return (x + xx * mu[j:j + 1, :]).astype(BF16)

    r = jnp.dot(mix(0), wr_ref[...], preferred_element_type=F32)
    k = jnp.dot(mix(2), wk_ref[...], preferred_element_type=F32)
    v = jnp.dot(mix(3), wv_ref[...], preferred_element_type=F32)
    lw = w0_ref[...] + _bdot(jnp.tanh(jnp.dot(mix(1), w1_ref[...], preferred_element_type=F32)), w2_ref[...])
    logw = -jax.nn.sigmoid(lw) * RW_DECAY_SCALE
    a = jax.nn.sigmoid(a0_ref[...] + _bdot(jnp.dot(mix(4), a1_ref[...], preferred_element_type=F32), a2_ref[...]))
    g = _bdot(jax.nn.sigmoid(jnp.dot(mix(5), g1_ref[...], preferred_element_type=F32)), g2_ref[...])

    kkr = k * kk_ref[...]
    lane = lax.broadcasted_iota(jnp.int32, (1, LANES), 1)
    m0 = lane < RW_HEAD
    slabs = []
    for s in range(x.shape[1] // LANES):
        q = kkr[:, s * LANES:(s + 1) * LANES]
        q2 = q * q
        n0 = jnp.sum(jnp.where(m0, q2, 0.0), axis=-1, keepdims=True)
        n1 = jnp.sum(jnp.where(m0, 0.0, q2), axis=-1, keepdims=True)
        nrm = jnp.maximum(jnp.sqrt(jnp.where(m0, n0, n1)), 1e-12)
        slabs.append(q / nrm)
    kk = jnp.concatenate(slabs, axis=1)

    r_out[...] = r
    lw_out[...] = logw
    k_out[...] = k * (1.0 + (a - 1.0) * ka_ref[...])
    v_out[...] = v
    kk_out[...] = kk
    b_out[...] = kk * a
    g_out[...] = g


def _pad_axis(w, axis, size):
    pad = [(0, 0)] * w.ndim
    pad[axis] = (0, size - w.shape[axis])
    return jnp.pad(w, pad)


def _rwkv_proj(x, p, *, tm=256):
    B, L, D = x.shape
    tm = min(tm, L)
    main, prev, nxt = _halo_specs(tm, D, L, lead=1)
    full2 = lambda shape: pl.BlockSpec(shape, lambda z, b, i: (0, 0))
    perz = lambda shape: pl.BlockSpec((None,) + shape, lambda z, b, i: (z, 0, 0))
    out_spec = pl.BlockSpec((None, None, tm, D), lambda z, b, i: (z, b, i, 0))
    w1 = _pad_axis(p['w1'], 2, RW_LORA_PAD).astype(BF16)
    w2 = _pad_axis(p['w2'], 1, RW_LORA_PAD).astype(BF16)
    a1 = _pad_axis(p['a1'], 2, RW_LORA_PAD).astype(BF16)
    a2 = _pad_axis(p['a2'], 1, RW_LORA_PAD).astype(BF16)
    g1 = _pad_axis(p['g1'], 2, RW_GATE_PAD).astype(BF16)
    g2 = _pad_axis(p['g2'], 1, RW_GATE_PAD).astype(BF16)
    outs = pl.pallas_call(
        _rwkv_proj_kernel,
        grid=(2, B, L // tm),
        in_specs=[main, prev, nxt, perz((6, D)), full2((D, D)), full2((D, D)), full2((D, D)),
                  perz((1, D)), perz((D, RW_LORA_PAD)), perz((RW_LORA_PAD, D)),
                  perz((1, D)), perz((D, RW_LORA_PAD)), perz((RW_LORA_PAD, D)),
                  perz((D, RW_GATE_PAD)), perz((RW_GATE_PAD, D)), full2((1, D)), full2((1, D))],
        out_specs=[out_spec] * 7,
        out_shape=[jax.ShapeDtypeStruct((2, B, L, D), F32)] * 7,
        compiler_params=_cparams(("parallel", "parallel", "parallel")),
        name="rwkv_proj",
    )(x, x, x, p['mu'], p['w_r'].astype(BF16), p['w_k'].astype(BF16), p['w_v'].astype(BF16),
      p['w0'].reshape(2, 1, D), w1, w2, p['a0'].reshape(2, 1, D), a1, a2, g1, g2,
      p['k_k'].reshape(1, D), p['k_a'].reshape(1, D))
    return outs


def _rw_out_ln_kernel(o_ref, w_ref, res_ref, g_ref, b_ref, out_ref):
    s = (o_ref[0] + o_ref[1]).astype(BF16)
    m = jnp.dot(s, w_ref[...], preferred_element_type=F32)
    out_ref[...] = _layer_norm_rows(DN_ALPHA * res_ref[...] + m, g_ref[...], b_ref[...])


def _rw_out_ln(o, w_o, res, ln_g, ln_b, *, tm=512):
    _, T, D = o.shape
    tm = min(tm, T)
    row = pl.BlockSpec((tm, D), lambda i: (i, 0))
    vec = pl.BlockSpec((1, D), lambda i: (0, 0))
    return pl.pallas_call(
        _rw_out_ln_kernel,
        grid=(T // tm,),
        in_specs=[pl.BlockSpec((2, tm, D), lambda i: (0, i, 0)), pl.BlockSpec((D, D), lambda i: (0, 0)),
                  row, vec, vec],
        out_specs=row,
        out_shape=jax.ShapeDtypeStruct((T, D), F32),
        compiler_params=_cparams(("parallel",)),
        name="rw_out_ln",
    )(o, w_o.astype(BF16), res, ln_g.reshape(1, D), ln_b.reshape(1, D))


def _rwkv_layer(x, p, ln_g, ln_b):
    B, L, D = x.shape
    r, lw, k, v, kk, b, g = _rwkv_proj(x, p)
    o = _rwkv_scan(r, lw, k, v, kk, b, g, p['r_k'].reshape(D), p['gn_g'], p['gn_b'])
    return _rw_out_ln(o.reshape(2, B * L, D), p['w_o'], x.reshape(B * L, D), ln_g, ln_b).reshape(B, L, D)


HY_U_W = (HY_ORDER + 1) * HY_W
MLA_HEAD_PAD = LANES
MLA_QK_W = MLA_HEADS * MLA_HEAD_PAD
MLA_V_W = MLA_HEADS * MLA_V
MLA_SCALE = (MLA_NOPE + MLA_ROPE) ** -0.5 * math.log2(math.e)
HALF_ROPE = MLA_ROPE // 2


def _rms_rows(c, g):
    return c * lax.rsqrt(jnp.mean(c * c, axis=-1, keepdims=True) + RMS_EPS) * g


def _in_proj_kernel(x_ref, wu_ref, ws_ref, wq_ref, wkv_ref, qn_ref, kvn_ref, cos_ref, sin_ref,
                    u_out, q_out, k_out, v_out):
    xb = x_ref[...].astype(BF16)
    u_out[...] = jnp.dot(xb, wu_ref[...], preferred_element_type=F32)
    small = jnp.dot(xb, ws_ref[...], preferred_element_type=F32)
    c_q = small[:, :MLA_Q_LORA]
    c_kv = small[:, MLA_Q_LORA:MLA_Q_LORA + MLA_KV_LORA]
    kpe_a = small[:, MLA_Q_LORA + MLA_KV_LORA:MLA_Q_LORA + MLA_KV_LORA + LANES]
    kpe_b = small[:, MLA_Q_LORA + MLA_KV_LORA + LANES:]
    cos = cos_ref[...]
    sin = sin_ref[...]
    cos_h = jnp.concatenate([cos] * MLA_HEADS, axis=1)
    sin_h = jnp.concatenate([sin] * MLA_HEADS, axis=1)

    qq = _bdot(_rms_rows(c_q, qn_ref[...]), wq_ref[...])
    q_out[...] = (qq[:, :MLA_QK_W] * cos_h + qq[:, MLA_QK_W:] * sin_h).astype(BF16)

    kv = _bdot(_rms_rows(c_kv, kvn_ref[...]), wkv_ref[...])
    kpe = kpe_a * cos + kpe_b * sin
    k_out[...] = (kv[:, :MLA_QK_W] + jnp.concatenate([kpe] * MLA_HEADS, axis=1)).astype(BF16)
    v_out[...] = kv[:, MLA_QK_W:].T.astype(BF16)


def _rope_head_tables(L):
    inv = ROPE_THETA ** (-jnp.arange(0, MLA_ROPE, 2, dtype=F32) / MLA_ROPE)
    ang = jnp.arange(L, dtype=F32)[:, None] * inv[None, :]
    cos, sin = jnp.cos(ang), jnp.sin(ang)
    ones = jnp.ones((L, MLA_NOPE), F32)
    zn = jnp.zeros((L, MLA_NOPE), F32)
    zp = jnp.zeros((L, MLA_HEAD_PAD - MLA_NOPE - MLA_ROPE), F32)
    return (jnp.concatenate([ones, cos, cos, zp], axis=1), jnp.concatenate([zn, sin, sin, zp], axis=1))


def _mla_weights(w_in, w_uq, w_ukv):
    s1 = HY_U_W
    s2 = s1 + MLA_Q_LORA
    s3 = s2 + MLA_KV_LORA
    kx1, kx2 = w_in[:, s3:s3 + HALF_ROPE], w_in[:, s3 + HALF_ROPE:]
    zl = jnp.zeros((w_in.shape[0], MLA_NOPE), F32)
    zr = jnp.zeros((w_in.shape[0], MLA_HEAD_PAD - MLA_NOPE - MLA_ROPE), F32)
    kpe_a = jnp.concatenate([zl, kx1, kx2, zr], axis=1)
    kpe_b = jnp.concatenate([zl, -kx2, kx1, zr], axis=1)
    w_small = jnp.concatenate([w_in[:, s1:s3], kpe_a, kpe_b], axis=1)
    q3 = w_uq.reshape(MLA_Q_LORA, MLA_HEADS, MLA_NOPE + MLA_ROPE) * MLA_SCALE
    nope, x1, x2 = q3[..., :MLA_NOPE], q3[..., MLA_NOPE:MLA_NOPE + HALF_ROPE], q3[..., MLA_NOPE + HALF_ROPE:]
    zq = jnp.zeros((MLA_Q_LORA, MLA_HEADS, MLA_HEAD_PAD - MLA_NOPE - MLA_ROPE), F32)
    wqa = jnp.concatenate([nope, x1, x2, zq], axis=-1).reshape(MLA_Q_LORA, MLA_QK_W)
    wqb = jnp.concatenate([jnp.zeros_like(nope), -x2, x1, zq], axis=-1).reshape(MLA_Q_LORA, MLA_QK_W)
    kv3 = w_ukv.reshape(MLA_KV_LORA, MLA_HEADS, MLA_NOPE + MLA_V)
    wkn = jnp.concatenate([kv3[..., :MLA_NOPE], jnp.zeros((MLA_KV_LORA, MLA_HEADS, MLA_HEAD_PAD - MLA_NOPE), F32)],
                          axis=-1).reshape(MLA_KV_LORA, MLA_QK_W)
    wv = kv3[..., MLA_NOPE:].reshape(MLA_KV_LORA, MLA_V_W)
    return (w_in[:, :s1].astype(BF16), w_small.astype(BF16), jnp.concatenate([wqa, wqb], axis=1).astype(BF16),
            jnp.concatenate([wkn, wv], axis=1).astype(BF16))


def _in_proj(x, w_in, w_uq, w_ukv, q_norm, kv_norm, *, tm=256):
    B, L, D = x.shape
    tm = min(tm, L)
    wu, ws, wq, wkv = _mla_weights(w_in, w_uq, w_ukv)
    cos, sin = _rope_head_tables(L)
    row = lambda w: pl.BlockSpec((None, tm, w), lambda b, i: (b, i, 0))
    full = lambda a: pl.BlockSpec(a.shape, lambda b, i: (0, 0))
    pos = pl.BlockSpec((tm, LANES), lambda b, i: (i, 0))
    qn = q_norm.reshape(1, -1)
    kvn = kv_norm.reshape(1, -1)
    return pl.pallas_call(
        _in_proj_kernel,
        grid=(B, L // tm),
        in_specs=[row(D), full(wu), full(ws), full(wq), full(wkv), full(qn), full(kvn), pos, pos],
        out_specs=[row(HY_U_W), row(MLA_QK_W), row(MLA_QK_W),
                   pl.BlockSpec((None, MLA_V_W, tm), lambda b, i: (b, 0, i))],
        out_shape=[jax.ShapeDtypeStruct((B, L, HY_U_W), F32), jax.ShapeDtypeStruct((B, L, MLA_QK_W), BF16),
                   jax.ShapeDtypeStruct((B, L, MLA_QK_W), BF16), jax.ShapeDtypeStruct((B, MLA_V_W, L), BF16)],
        compiler_params=_cparams(("parallel", "parallel")),
        name="in_proj",
    )(x, wu, ws, wq, wkv, qn, kvn, cos, sin)


MLA_PAIR = 2
ATTN_SUM_ROWS = 8


def _attn_kernel(q_ref, k_ref, vt_ref, o_ref, s_ref, p_ref, mx_ref, al_ref, m_ref, l_ref, acc_ref, *, tk):
    tq = q_ref.shape[0]
    nk = k_ref.shape[0] // tk
    heads = range(MLA_PAIR)
    head_lanes = lambda h: slice(h * MLA_HEAD_PAD, (h + 1) * MLA_HEAD_PAD)
    head_rows = lambda h: slice(h * MLA_V, (h + 1) * MLA_V)

    m_ref[...] = jnp.full_like(m_ref, -jnp.inf)
    l_ref[...] = jnp.zeros_like(l_ref)
    acc_ref[...] = jnp.zeros_like(acc_ref)

    ones = jnp.ones((ATTN_SUM_ROWS, tk), BF16)

    p_ref[1] = jnp.zeros_like(p_ref[1])
    al_ref[1] = jnp.ones_like(al_ref[1])

    def scores(j, buf):
        keys = pl.ds(pl.multiple_of(j * tk, tk), tk)
        for h in heads:
            st = lax.dot_general(k_ref[keys, head_lanes(h)], q_ref[:, head_lanes(h)],
                                 (((1,), (1,)), ((), ())), preferred_element_type=F32)
            s_ref[buf, h] = st
            mx_ref[buf, h] = jnp.max(st, axis=0, keepdims=True)

    def softmax(buf):
        for h in heads:
            m_old = m_ref[h]
            m_new = jnp.maximum(m_old, mx_ref[buf, h])
            al_ref[buf, h] = jnp.exp2(m_old - m_new)
            m_ref[h] = m_new
            p_ref[buf, h] = jnp.exp2(s_ref[buf, h] - m_new).astype(BF16)

    def values(j, buf):
        keys = pl.ds(pl.multiple_of(j * tk, tk), tk)
        for h in heads:
            p = p_ref[buf, h]
            alpha = al_ref[buf, h]
            pv = jnp.dot(vt_ref[head_rows(h), keys], p, preferred_element_type=F32)
            acc_ref[head_rows(h), :] = alpha * acc_ref[head_rows(h), :] + pv
            psum = jnp.dot(ones, p, preferred_element_type=F32)
            l_ref[h] = alpha * l_ref[h] + psum[0:1, :]

    scores(0, 0)

    def pair(i, carry):
        for u in range(0, ATTN_UNROLL, 2):
            j = ATTN_UNROLL * i + u
            softmax(0)
            scores(j + 1, 1)
            values(jnp.maximum(j - 1, 0), 1)
            softmax(1)
            scores(jnp.minimum(j + 2, nk - 1), 0)
            values(j, 0)
        return carry

    lax.fori_loop(0, nk // ATTN_UNROLL, pair, 0)
    values(nk - 1, 1)

    inv = jnp.concatenate([jnp.broadcast_to(1.0 / l_ref[h], (MLA_V, tq)) for h in heads], axis=0)
    o_ref[...] = (acc_ref[...] * inv).T


ATTN_UNROLL = 2


def _attention(q, k, vt, *, tq=512, tk=1024):
    B, L, _ = q.shape
    tq, tk = min(tq, L), min(tk, L // ATTN_UNROLL)
    pairs = MLA_HEADS // MLA_PAIR
    return pl.pallas_call(
        functools.partial(_attn_kernel, tk=tk),
        grid=(B, pairs, L // tq),
        in_specs=[pl.BlockSpec((None, tq, MLA_PAIR * MLA_HEAD_PAD), lambda b, h, qi: (b, qi, h)),
                  pl.BlockSpec((None, L, MLA_PAIR * MLA_HEAD_PAD), lambda b, h, qi: (b, 0, h)),
                  pl.BlockSpec((None, LANES, L), lambda b, h, qi: (b, h, 0))],
        out_specs=pl.BlockSpec((None, tq, LANES), lambda b, h, qi: (b, qi, h)),
        out_shape=jax.ShapeDtypeStruct((B, L, MLA_V_W), F32),
        scratch_shapes=[pltpu.VMEM((2, MLA_PAIR, tk, tq), F32), pltpu.VMEM((2, MLA_PAIR, tk, tq), BF16),
                        pltpu.VMEM((2, MLA_PAIR, 1, tq), F32), pltpu.VMEM((2, MLA_PAIR, 1, tq), F32),
                        pltpu.VMEM((MLA_PAIR, 1, tq), F32),
                        pltpu.VMEM((MLA_PAIR, 1, tq), F32), pltpu.VMEM((LANES, tq), F32)],
        compiler_params=_cparams(("parallel", "parallel", "parallel")),
        name="attention",
    )(q, k, vt)


def _split3_rows(x):
    hi = x.astype(BF16)
    lo = (x - hi.astype(F32)).astype(BF16)
    return jnp.concatenate([hi, lo, hi], axis=0)


def _split3_cols(m):
    hi = m.astype(BF16)
    lo = (m - hi.astype(F32)).astype(BF16)
    return jnp.concatenate([hi, hi, lo], axis=-1)


def _fft_factor(n):
    n1 = 1 << ((n.bit_length() - 1 + 1) // 2)
    return n1, n // n1


def _dft_tables(L):
    N = 2 * L
    N1, N2 = _fft_factor(N)
    h = N1 // 2
    k1 = jnp.arange(N1, dtype=jnp.int32)
    ang1 = (-2.0 * math.pi / N1) * ((k1[:, None] * k1[None, :]) % N1).astype(F32)
    f1r, f1i = jnp.cos(ang1), jnp.sin(ang1)
    lead_pair = jnp.concatenate([jnp.concatenate([f1r[:, :h], -f1i[:, :h]], axis=1),
                                 jnp.concatenate([f1i[:, :h], f1r[:, :h]], axis=1)], axis=0)
    m1 = jnp.arange(h, dtype=jnp.int32)
    cols_gen, cols_first = N1 - 1 - m1, (N1 - m1) % N1
    has_lag = (m1 > 0).astype(F32)[None, :]
    filt_fwd = jnp.concatenate([f1r[:, :h], f1i[:, :h]], axis=0)
    filt_bwd = jnp.stack([jnp.concatenate([f1r[:, cols_gen], f1i[:, cols_gen]], axis=0),
                          jnp.concatenate([f1r[:, cols_first] * has_lag, f1i[:, cols_first] * has_lag], axis=0)])
    f1rt, f1it = f1r.T[:h], f1i.T[:h]
    lead_inv = jnp.concatenate([jnp.concatenate([f1rt, f1it], axis=1),
                                jnp.concatenate([-f1it, f1rt], axis=1)], axis=0) * (1.0 / N)
    n2 = jnp.arange(N2, dtype=jnp.int32)
    freq = k1[:, None, None] + N1 * n2[None, :, None]
    idx = (freq * n2[None, None, :]) % N
    ang2 = (-2.0 * math.pi / N) * idx.astype(F32)
    gr, gi = jnp.cos(ang2), jnp.sin(ang2)
    inner_fwd = jnp.concatenate([jnp.concatenate([gr, -gi], axis=2),
                                 jnp.concatenate([gi, gr], axis=2)], axis=1)
    grt, git = jnp.swapaxes(gr, 1, 2), jnp.swapaxes(gi, 1, 2)
    inner_inv = jnp.concatenate([jnp.concatenate([grt, git], axis=2),
                                 jnp.concatenate([-git, grt], axis=2)], axis=1)
    return dict(N1=N1, N2=N2, lead_pair=_split3_cols(lead_pair), filt_fwd=_split3_cols(filt_fwd),
                filt_bwd=_split3_cols(filt_bwd), lead_inv=_split3_cols(lead_inv), inner_fwd=_split3_cols(inner_fwd),
                inner_inv=_split3_cols(inner_inv))


FFT_LANE_TILE = 1024


def _dft_lead_kernel(m_ref, x_ref, o_ref):
    k = m_ref.shape[1] // 3
    x = x_ref[...].reshape(k, x_ref.shape[-1])
    o_ref[...] = jnp.dot(m_ref[...], _split3_rows(x), preferred_element_type=F32).reshape(o_ref.shape)


def _dft_lead_gate_kernel(m_ref, x_ref, z_ref, g_ref, sk_ref, o_ref):
    k = m_ref.shape[1] // 3
    x = x_ref[...].reshape(k, x_ref.shape[-1])
    conv = jnp.dot(m_ref[...], _split3_rows(x), preferred_element_type=F32).reshape(o_ref.shape)
    o_ref[...] = g_ref[...] * (conv + sk_ref[...] * z_ref[...])


def _dft_inner_conv_kernel(mf_ref, mi_ref, a_ref, kf_ref, o_ref):
    n2 = a_ref.shape[1]
    a = a_ref[...].reshape(2 * n2, a_ref.shape[-1])
    x = jnp.dot(mf_ref[...], _split3_rows(a), preferred_element_type=F32)
    xr, xi = x[:n2], x[n2:]
    kr, ki = kf_ref[0], kf_ref[1]
    y = jnp.concatenate([xr * kr - xi * ki, xr * ki + xi * kr], axis=0)
    o_ref[...] = jnp.dot(mi_ref[...], _split3_rows(y), preferred_element_type=F32).reshape(o_ref.shape)


def _dft_inner_filter_kernel(mf_ref, a_ref, s_ref, o_ref):
    n2 = a_ref.shape[1]
    a = a_ref[...].reshape(2 * n2, a_ref.shape[-1])
    x = jnp.dot(mf_ref[...], _split3_rows(a), preferred_element_type=F32) * s_ref[...]
    o_ref[...] = x.reshape(o_ref.shape)


def _dft_lead_filter_kernel(mf_ref, mb_ref, fwd_ref, bwd_ref, o_ref):
    first = jnp.where(pl.program_id(0) == 0, 1, 0)
    a = (jnp.dot(mf_ref[...], _split3_rows(fwd_ref[...]), preferred_element_type=F32)
         + jnp.dot(mb_ref[first], _split3_rows(bwd_ref[...]), preferred_element_type=F32))
    o_ref[...] = a.reshape(o_ref.shape)


def _filter_spectrum(hf, inv_norm, tabs):
    L, Cf = hf.shape[0], hf.shape[1] // 2
    N1, N2 = tabs['N1'], tabs['N2']
    h = N1 // 2
    hv = hf.reshape(h, N2 * 2 * Cf)
    a = pl.pallas_call(
        _dft_lead_filter_kernel,
        grid=(N2,),
        in_specs=[pl.BlockSpec((2 * N1, 3 * h), lambda j: (0, 0)), pl.BlockSpec((2, 2 * N1, 3 * h), lambda j: (0, 0, 0)),
                  pl.BlockSpec((h, Cf), lambda j: (0, 2 * j)),
                  pl.BlockSpec((h, Cf), lambda j: (0, 2 * lax.rem(N2 - j, N2) + 1))],
        out_specs=pl.BlockSpec((2, N1, Cf), lambda j: (0, 0, j)),
        out_shape=jax.ShapeDtypeStruct((2, N1, N2 * Cf), F32),
        compiler_params=_cparams(("arbitrary",)),
        name="hyena_filter_dft_lead",
    )(tabs['filt_fwd'], tabs['filt_bwd'], hv, hv)
    return pl.pallas_call(
        _dft_inner_filter_kernel,
        grid=(N1,),
        in_specs=[pl.BlockSpec((None, 2 * N2, 6 * N2), lambda k1: (k1, 0, 0)),
                  pl.BlockSpec((2, None, N2, Cf), lambda k1: (0, k1, 0, 0)),
                  pl.BlockSpec((1, Cf), lambda k1: (0, 0))],
        out_specs=pl.BlockSpec((2, None, N2, Cf), lambda k1: (0, k1, 0, 0)),
        out_shape=jax.ShapeDtypeStruct((2, N1, N2, Cf), F32),
        compiler_params=_cparams(("parallel",)),
        name="hyena_filter_dft_inner",
    )(tabs['inner_fwd'], a.reshape(2, N1, N2, Cf), inv_norm)


def _long_conv_gate(z, gate, skip, kspec, order, tabs):
    B, L, C = z.shape
    N1, N2 = tabs['N1'], tabs['N2']
    h = N1 // 2
    P = B // 2
    W = N2 * C
    wc = min(FFT_LANE_TILE, W)
    zv = z.reshape(B, h, W)
    a = pl.pallas_call(
        _dft_lead_kernel,
        grid=(P, W // wc),
        in_specs=[pl.BlockSpec((2 * N1, 3 * N1), lambda p, j: (0, 0)),
                  pl.BlockSpec((2, h, wc), lambda p, j: (p, 0, j))],
        out_specs=pl.BlockSpec((None, 2, N1, wc), lambda p, j: (p, 0, 0, j)),
        out_shape=jax.ShapeDtypeStruct((P, 2, N1, W), F32),
        compiler_params=_cparams(("parallel", "parallel")),
        name="hyena_dft_lead",
    )(tabs['lead_pair'], zv)
    inner = pl.BlockSpec((None, 2, None, N2, C), lambda p, k1: (p, 0, k1, 0, 0))
    mat = pl.BlockSpec((None, 2 * N2, 6 * N2), lambda p, k1: (k1, 0, 0))
    bm = pl.pallas_call(
        _dft_inner_conv_kernel,
        grid=(P, N1),
        in_specs=[mat, mat, inner, pl.BlockSpec((2, None, N2, C), lambda p, k1: (0, k1, 0, order))],
        out_specs=inner,
        out_shape=jax.ShapeDtypeStruct((P, 2, N1, N2, C), F32),
        compiler_params=_cparams(("parallel", "parallel")),
        name="hyena_dft_inner",
    )(tabs['inner_fwd'], tabs['inner_inv'], a.reshape(P, 2, N1, N2, C), kspec)
    pair = pl.BlockSpec((2, h, wc), lambda p, j: (p, 0, j))
    sk = jnp.tile(skip.reshape(1, C), (1, wc // C))
    out = pl.pallas_call(
        _dft_lead_gate_kernel,
        grid=(P, W // wc),
        in_specs=[pl.BlockSpec((N1, 6 * N1), lambda p, j: (0, 0)),
                  pl.BlockSpec((None, 2, N1, wc), lambda p, j: (p, 0, 0, j)), pair, pair,
                  pl.BlockSpec((1, wc), lambda p, j: (0, 0))],
        out_specs=pair,
        out_shape=jax.ShapeDtypeStruct((B, h, W), F32),
        compiler_params=_cparams(("parallel", "parallel")),
        name="hyena_dft_lead_inv",
    )(tabs['lead_inv'], bm.reshape(P, 2, N1, W), zv, gate.reshape(B, h, W), sk)
    return out.reshape(B, L, C)


def _short_conv_kernel(u_ref, up_ref, un_ref, cw_ref, cb_ref, z_out, x1_out, x2_out):
    i = pl.program_id(1)
    u = u_ref[...]
    u_prev, u_next = _shifted(u, up_ref, un_ref, i, pl.num_programs(1))
    cw = cw_ref[...]
    y = u_prev * cw[0:1, :] + u * cw[1:2, :] + u_next * cw[2:3, :] + cb_ref[...]
    z_out[...] = y[:, :HY_W]
    x1_out[...] = y[:, HY_W:2 * HY_W]
    x2_out[...] = y[:, 2 * HY_W:]


def _short_conv(u, conv_w, conv_b, *, tm=512):
    B, L, W = u.shape
    tm = min(tm, L)
    main, prev, nxt = _halo_specs(tm, W, L, lead=0)
    part = pl.BlockSpec((None, tm, HY_W), lambda b, i: (b, i, 0))
    return pl.pallas_call(
        _short_conv_kernel,
        grid=(B, L // tm),
        in_specs=[main, prev, nxt, pl.BlockSpec((3, W), lambda b, i: (0, 0)), pl.BlockSpec((1, W), lambda b, i: (0, 0))],
        out_specs=[part] * 3,
        out_shape=[jax.ShapeDtypeStruct((B, L, HY_W), F32)] * 3,
        compiler_params=_cparams(("parallel", "parallel")),
        name="hyena_short_conv",
    )(u, u, u, conv_w, conv_b.reshape(1, W))


HY_FILT_HID = 64
HY_FILT_OUT = 2 * HY_ORDER * HY_W
TWO_PI = 2.0 * math.pi


def _hdot(a, b):
    return jnp.dot(a, b, precision=lax.Precision.HIGHEST, preferred_element_type=F32)


def _filter_mlp_kernel(bands_ref, w1t_ref, w1s_ref, w1c_ref, b1_ref, w2_ref, b2_ref, w3_ref, fr_ref, dec_ref,
                       h_out, abs_out, *, seq_len):
    i = pl.program_id(0)
    tm = h_out.shape[0]
    row = i * tm + lax.broadcasted_iota(jnp.int32, (tm, 1), 0)
    t = row.astype(F32) / seq_len
    ph = (TWO_PI * t) * bands_ref[...]
    fr = fr_ref[...]
    pre1 = t * w1t_ref[...] + _hdot(jnp.sin(ph), w1s_ref[...]) + _hdot(jnp.cos(ph), w1c_ref[...]) + b1_ref[...]
    h1 = jnp.sin(fr * pre1)
    h2 = jnp.sin(fr * (_hdot(h1, w2_ref[...]) + b2_ref[...]))
    hf = _hdot(h2, w3_ref[...]) * jnp.exp(-t * jnp.abs(dec_ref[...]))
    h_out[...] = hf

    @pl.when(i == 0)
    def _():
        abs_out[...] = jnp.zeros_like(abs_out)

    col = lax.broadcasted_iota(jnp.int32, (1, HY_FILT_OUT), 1)
    used = jnp.where(jnp.logical_and(row == 0, col >= HY_FILT_OUT // 2), 0.0, jnp.abs(hf))
    abs_out[...] += jnp.sum(used, axis=0, keepdims=True)


def _hyena_filter_spectrum(L, w1, b1, w2, b2, w3, freq, decay, tabs, *, tm=256):
    tm = min(tm, L)
    bands = _pad_axis(jnp.linspace(1e-4, HY_BANDS - 1, HY_BANDS, dtype=F32).reshape(1, HY_BANDS), 1, LANES)
    w1p = _pad_axis(w1, 1, LANES)
    w1t = w1p[0:1]
    w1s = _pad_axis(w1p[1:1 + HY_BANDS], 0, LANES)
    w1c = _pad_axis(w1p[1 + HY_BANDS:], 0, LANES)
    pad_vec = lambda v: _pad_axis(v.reshape(1, -1), 1, LANES)
    w2p = _pad_axis(_pad_axis(w2, 0, LANES), 1, LANES)
    w3p = _pad_axis(w3, 0, LANES)
    full = lambda a: pl.BlockSpec(a.shape, lambda i: (0, 0))
    args = (bands, w1t, w1s, w1c, pad_vec(b1), w2p, pad_vec(b2), w3p, pad_vec(freq), decay.reshape(1, HY_FILT_OUT))
    hf, abs_sum = pl.pallas_call(
        functools.partial(_filter_mlp_kernel, seq_len=L),
        grid=(L // tm,),
        in_specs=[full(a) for a in args],
        out_specs=[pl.BlockSpec((tm, HY_FILT_OUT), lambda i: (i, 0)), pl.BlockSpec((1, HY_FILT_OUT), lambda i: (0, 0))],
        out_shape=[jax.ShapeDtypeStruct((L, HY_FILT_OUT), F32), jax.ShapeDtypeStruct((1, HY_FILT_OUT), F32)],
        compiler_params=_cparams(("arbitrary",)),
        name="hyena_filter_mlp",
    )(*args)
    half = HY_FILT_OUT // 2
    inv_norm = 1.0 / (abs_sum[:, :half] + abs_sum[:, half:] + 1e-6)
    return _filter_spectrum(hf, inv_norm, tabs)


def _hyena(u, conv_w, conv_b, skip, kspec, tabs):
    z, x1, x2 = _short_conv(u, conv_w, conv_b)
    z = _long_conv_gate(z, x1, skip[0], kspec, 0, tabs)
    return _long_conv_gate(z, x2, skip[1], kspec, 1, tabs)


def _mix_out_ln_kernel(a_ref, b_ref, wa_ref, wb_ref, res_ref, g_ref, bb_ref, out_ref):
    m = (jnp.dot(a_ref[...].astype(BF16), wa_ref[...], preferred_element_type=F32)
         + jnp.dot(b_ref[...].astype(BF16), wb_ref[...], preferred_element_type=F32))
    out_ref[...] = _layer_norm_rows(DN_ALPHA * res_ref[...] + m, g_ref[...], bb_ref[...])


def _mix_out_ln(y_hy, y_mla, w_out, res, ln_g, ln_b, *, tm=512):
    T, D = res.shape
    tm = min(tm, T)
    wa = w_out[:HY_W].astype(BF16)
    wb = w_out[HY_W:].astype(BF16)
    row = lambda w: pl.BlockSpec((tm, w), lambda i: (i, 0))
    full = lambda a: pl.BlockSpec(a.shape, lambda i: (0, 0))
    vec = pl.BlockSpec((1, D), lambda i: (0, 0))
    return pl.pallas_call(
        _mix_out_ln_kernel,
        grid=(T // tm,),
        in_specs=[row(HY_W), row(MLA_V_W), full(wa), full(wb), row(D), vec, vec],
        out_specs=row(D),
        out_shape=jax.ShapeDtypeStruct((T, D), F32),
        compiler_params=_cparams(("parallel",)),
        name="mix_out_ln",
    )(y_hy, y_mla, wa, wb, res, ln_g.reshape(1, D), ln_b.reshape(1, D))


def _hyena_mla_layer(x, p, kspec, tabs, ln_g, ln_b):
    B, L, D = x.shape
    u, q, k, v = _in_proj(x, p['w_in'], p['w_uq'], p['w_ukv'], p['q_norm'], p['kv_norm'])
    y_hy = _hyena(u, p['conv_w'], p['conv_b'], p['skip'], kspec, tabs)
    y_mla = _attention(q, k, v)
    return _mix_out_ln(y_hy.reshape(B * L, HY_W), y_mla.reshape(B * L, MLA_V_W), p['w_out'],
                       x.reshape(B * L, D), ln_g, ln_b).reshape(B, L, D)


MOE_ROWS = 256
ROUTE_W = LANES
NEG_BIG = -1e30


def _router_kernel(x_ref, w_ref, idx_out, gate_out, rank_out, cnt_out, cnt_ref):
    logits = _hdot(x_ref[...], w_ref[...])
    lane = lax.broadcasted_iota(jnp.int32, logits.shape, 1)
    is_group = lane < MOE_GROUPS
    gl = jnp.where(is_group, logits, NEG_BIG)
    gmax = jnp.max(gl, axis=-1, keepdims=True)
    g_top = jnp.min(jnp.where(gl == gmax, lane, ROUTE_W), axis=-1, keepdims=True)
    g_prob = 1.0 / jnp.sum(jnp.where(is_group, jnp.exp(gl - gmax), 0.0), axis=-1, keepdims=True)

    first = MOE_GROUPS + g_top * MOE_PER_GROUP
    in_grp = jnp.logical_and(lane >= first, lane < first + MOE_PER_GROUP)
    el = jnp.where(in_grp, logits, NEG_BIG)
    emax = jnp.max(el, axis=-1, keepdims=True)
    ex = jnp.where(in_grp, jnp.exp(el - emax), 0.0)
    prob = ex / jnp.sum(ex, axis=-1, keepdims=True)
    pm = jnp.where(in_grp, prob, -1.0)
    p1 = jnp.max(pm, axis=-1, keepdims=True)
    l1 = jnp.min(jnp.where(pm == p1, lane, ROUTE_W), axis=-1, keepdims=True)
    pm2 = jnp.where(lane == l1, -1.0, pm)
    p2 = jnp.max(pm2, axis=-1, keepdims=True)
    l2 = jnp.min(jnp.where(pm2 == p2, lane, ROUTE_W), axis=-1, keepdims=True)
    tot = p1 + p2
    e1 = l1 - MOE_GROUPS
    e2 = l2 - MOE_GROUPS
    idx_out[...] = jnp.where(lane == 0, e1, jnp.where(lane == 1, e2, 0))
    gate_out[...] = jnp.where(lane == 0, g_prob * (p1 / tot), jnp.where(lane == 1, g_prob * (p2 / tot), 0.0))

    @pl.when(pl.program_id(0) == 0)
    def _():
        cnt_ref[...] = jnp.zeros_like(cnt_ref)

    tm = logits.shape[0]
    oh1 = jnp.where(lane == l1, 1.0, 0.0)
    oh2 = jnp.where(lane == l2, 1.0, 0.0)
    both = oh1 + oh2
    ri = lax.broadcasted_iota(jnp.int32, (tm, tm), 0)
    ci = lax.broadcasted_iota(jnp.int32, (tm, tm), 1)
    earlier = jnp.where(ci < ri, 1.0, 0.0).astype(BF16)
    before = cnt_ref[...] + jnp.dot(earlier, both.astype(BF16), preferred_element_type=F32)
    r1 = jnp.sum(oh1 * before, axis=-1, keepdims=True)
    r2 = jnp.sum(oh2 * before, axis=-1, keepdims=True)
    rank_out[...] = jnp.where(lane == 0, r1, jnp.where(lane == 1, r2, 0.0)).astype(jnp.int32)
    cnt_ref[...] += jnp.sum(both, axis=0, keepdims=True)
    cnt_out[...] = cnt_ref[...]


def _router(xt, w_group, w_expert, *, tm=512):
    T, D = xt.shape
    tm = min(tm, T)
    w = _pad_axis(jnp.concatenate([w_group, w_expert], axis=1), 1, ROUTE_W)
    row = pl.BlockSpec((tm, ROUTE_W), lambda i: (i, 0))
    one = pl.BlockSpec((1, ROUTE_W), lambda i: (0, 0))
    return pl.pallas_call(
        _router_kernel,
        grid=(T // tm,),
        in_specs=[pl.BlockSpec((tm, D), lambda i: (i, 0)), pl.BlockSpec((D, ROUTE_W), lambda i: (0, 0))],
        out_specs=[row, row, row, one],
        out_shape=[jax.ShapeDtypeStruct((T, ROUTE_W), jnp.int32), jax.ShapeDtypeStruct((T, ROUTE_W), F32),
                   jax.ShapeDtypeStruct((T, ROUTE_W), jnp.int32), jax.ShapeDtypeStruct((1, ROUTE_W), F32)],
        scratch_shapes=[pltpu.VMEM((1, ROUTE_W), F32)],
        compiler_params=_cparams(("arbitrary",)),
        name="moe_router",
    )(xt, w)


MOE_TILE = 256


def _tile_major(dest, tm):
    nt = dest.shape[0] // tm
    return dest.reshape(nt, tm, MOE_TOPK).transpose(0, 2, 1).reshape(nt, 1, MOE_TOPK * tm)


def _dispatch_kernel(dest_ref, x_ref, rows_in, rows_out, sem):
    del rows_in
    tm = x_ref.shape[0]

    def row_copy(k, r):
        return pltpu.make_async_copy(x_ref.at[pl.ds(r, 1)], rows_out.at[pl.ds(dest_ref[0, k * tm + r], 1)], sem)

    def start(r, c):
        for k in range(MOE_TOPK):
            row_copy(k, r).start()
        return c

    def wait(r, c):
        for k in range(MOE_TOPK):
            row_copy(k, r).wait()
        return c

    lax.fori_loop(0, tm, start, 0, unroll=8)
    lax.fori_loop(0, tm, wait, 0, unroll=8)


def _dispatch(xt, dest, n_rows):
    T, D = xt.shape
    tm = min(MOE_TILE, T)
    return pl.pallas_call(
        _dispatch_kernel,
        grid=(T // tm,),
        in_specs=[pl.BlockSpec((None, 1, MOE_TOPK * tm), lambda i: (i, 0, 0), memory_space=pltpu.SMEM),
                  pl.BlockSpec((tm, D), lambda i: (i, 0)), pl.BlockSpec(memory_space=pl.ANY)],
        out_specs=pl.BlockSpec(memory_space=pl.ANY),
        out_shape=jax.ShapeDtypeStruct((n_rows, D), F32),
        scratch_shapes=[pltpu.SemaphoreType.DMA(())],
        input_output_aliases={2: 0},
        compiler_params=_cparams(("arbitrary",)),
        name="moe_dispatch",
    )(_tile_major(dest, tm), xt, jnp.zeros((n_rows, D), F32))


def _expert_kernel(blk_e_ref, n_used_ref, x_ref, w1_ref, w3_ref, w2_ref, o_ref):
    i = pl.program_id(0)

    @pl.when(i < n_used_ref[0])
    def _():
        xb = x_ref[...].astype(BF16)
        h1 = jnp.dot(xb, w1_ref[...], preferred_element_type=F32)
        h3 = jnp.dot(xb, w3_ref[...], preferred_element_type=F32)
        h = (h1 * jax.nn.sigmoid(h1) * h3).astype(BF16)
        o_ref[...] = jnp.dot(h, w2_ref[...], preferred_element_type=F32)

    @pl.when(i >= n_used_ref[0])
    def _():
        o_ref[...] = jnp.zeros_like(o_ref)


def _experts(rows, blk_e, n_used, w1, w3, w2):
    P, D = rows.shape
    n_blocks = blk_e.shape[0]
    blk = lambda i, be, nu: (jnp.minimum(i, nu[0] - 1), 0)
    wspec = lambda shape: pl.BlockSpec((None,) + shape, lambda i, be, nu: (be[jnp.minimum(i, nu[0] - 1)], 0, 0))
    return pl.pallas_call(
        _expert_kernel,
        grid_spec=pltpu.PrefetchScalarGridSpec(
            num_scalar_prefetch=2,
            grid=(n_blocks,),
            in_specs=[pl.BlockSpec((MOE_ROWS, D), blk), wspec((D, MOE_FF)), wspec((D, MOE_FF)), wspec((MOE_FF, D))],
            out_specs=pl.BlockSpec((MOE_ROWS, D), lambda i, be, nu: (i, 0)),
        ),
        out_shape=jax.ShapeDtypeStruct((P, D), F32),
        compiler_params=_cparams(("arbitrary",)),
        name="moe_experts",
    )(blk_e, n_used, rows, w1, w3, w2)


def _combine_ln_kernel(dest_ref, dest_next_ref, eo_hbm, gate_ref, x_ref, g_ref, b_ref, out_ref, buf, sems):
    i = pl.program_id(0)
    tm = x_ref.shape[0]
    slot = lax.rem(i, 2)
    base = lambda s, k: pl.multiple_of((s * MOE_TOPK + k) * tm, tm)

    def row_copy(idx_ref, k, r, s):
        return pltpu.make_async_copy(eo_hbm.at[pl.ds(idx_ref[0, k * tm + r], 1)],
                                     buf.at[pl.ds(base(s, k) + r, 1)], sems.at[s])

    def start_all(idx_ref, s):
        def body(r, c):
            for k in range(MOE_TOPK):
                row_copy(idx_ref, k, r, s).start()
            return c
        lax.fori_loop(0, tm, body, 0, unroll=8)

    @pl.when(i == 0)
    def _():
        start_all(dest_ref, slot)

    @pl.when(i + 1 < pl.num_programs(0))
    def _():
        start_all(dest_next_ref, 1 - slot)

    def wait(r, c):
        for k in range(MOE_TOPK):
            row_copy(dest_ref, k, r, slot).wait()
        return c

    lax.fori_loop(0, tm, wait, 0, unroll=8)
    gate = gate_ref[...]
    f = gate[:, 0:1] * buf[pl.ds(base(slot, 0), tm), :] + gate[:, 1:2] * buf[pl.ds(base(slot, 1), tm), :]
    out_ref[...] = _layer_norm_rows(DN_ALPHA * x_ref[...] + f, g_ref[...], b_ref[...])


def _combine_ln(eo, dest, gate, xt, ln_g, ln_b):
    T, D = xt.shape
    tm = min(MOE_TILE, T)
    nt = T // tm
    dest_t = _tile_major(dest, tm)
    row = pl.BlockSpec((tm, D), lambda i: (i, 0))
    vec = pl.BlockSpec((1, D), lambda i: (0, 0))
    idx = lambda off: pl.BlockSpec((None, 1, MOE_TOPK * tm), lambda i: (jnp.minimum(i + off, nt - 1), 0, 0),
                                   memory_space=pltpu.SMEM)
    return pl.pallas_call(
        _combine_ln_kernel,
        grid=(nt,),
        in_specs=[idx(0), idx(1), pl.BlockSpec(memory_space=pl.ANY), pl.BlockSpec((tm, ROUTE_W), lambda i: (i, 0)),
                  row, vec, vec],
        out_specs=row,
        out_shape=jax.ShapeDtypeStruct((T, D), F32),
        scratch_shapes=[pltpu.VMEM((2 * MOE_TOPK * tm, D), F32), pltpu.SemaphoreType.DMA((2,))],
        compiler_params=_cparams(("arbitrary",)),
        name="moe_combine_ln",
    )(dest_t, dest_t, eo, gate, xt, ln_g.reshape(1, D), ln_b.reshape(1, D))


def _dispatch_plan(expert, rank, counts, n_blocks):
    cnt = counts[0, MOE_GROUPS:MOE_GROUPS + MOE_EXPERTS].astype(jnp.int32)
    padded = (cnt + MOE_ROWS - 1) // MOE_ROWS * MOE_ROWS
    pad_end = jnp.cumsum(padded)
    pad_start = pad_end - padded
    dest = jnp.take(pad_start, expert, axis=0) + rank
    first_row = jnp.arange(n_blocks, dtype=jnp.int32) * MOE_ROWS
    blk_e = jnp.minimum(jnp.sum(first_row[:, None] >= pad_end[None, :], axis=1), MOE_EXPERTS - 1).astype(jnp.int32)
    n_used = (pad_end[-1] // MOE_ROWS).astype(jnp.int32).reshape(1)
    return dest.astype(jnp.int32), blk_e, n_used


def _moe_layer(x, w_group, w_expert, w1, w3, w2, ln_g, ln_b):
    B, L, D = x.shape
    xt = x.reshape(B * L, D)
    T = B * L
    n_blocks = T * MOE_TOPK // MOE_ROWS + MOE_EXPERTS
    idx, gate, rank, counts = _router(xt, w_group, w_expert)
    dest, blk_e, n_used = _dispatch_plan(idx[:, :MOE_TOPK], rank[:, :MOE_TOPK], counts, n_blocks)
    rows = _dispatch(xt, dest, n_blocks * MOE_ROWS)
    eo = _experts(rows, blk_e, n_used, w1, w3, w2)
    return _combine_ln(eo, dest, gate, xt, ln_g, ln_b).reshape(B, L, D)


def _trunk(x, p, tabs, kspecs):
    for layer in range(DEPTH):
        i = layer // 2
        if layer % 2 == 0:
            lp = {n: p[n][i] for n in ('w_in', 'w_uq', 'w_ukv', 'q_norm', 'kv_norm', 'conv_w', 'conv_b', 'skip', 'w_out')}
            x = _hyena_mla_layer(x, lp, kspecs[i], tabs, p['ln1_g'][layer], p['ln1_b'][layer])
        else:
            lp = {n[3:]: p[n][i] for n in p if n.startswith('rw_')}
            x = _rwkv_layer(x, lp, p['ln1_g'][layer], p['ln1_b'][layer])
        x = _moe_layer(x, p['moe_w_group'][layer], p['moe_w_expert'][layer], p['moe_w1'][layer],
                       p['moe_w3'][layer], p['moe_w2'][layer], p['ln2_g'][layer], p['ln2_b'][layer])
    return x


def kernel(x_prompt, x_sample, ln1_g, ln1_b, ln2_g, ln2_b, mix_w_in, hy_conv_w, hy_conv_b, hy_ffn_w1,
           hy_ffn_b1, hy_ffn_w2, hy_ffn_b2, hy_ffn_w3, hy_sin_freq, hy_decay, hy_skip, mla_q_norm, mla_w_uq,
           mla_kv_norm, mla_w_ukv, mix_w_out, rw_mu, rw_w_r, rw_w_k, rw_w_v, rw_w0, rw_w1, rw_w2, rw_a0,
           rw_a1, rw_a2, rw_g1, rw_g2, rw_k_k, rw_k_a, rw_r_k, rw_gn_g, rw_gn_b, rw_w_o, moe_w_group,
           moe_w_expert, moe_w1, moe_w3, moe_w2):
    p = dict(ln1_g=ln1_g, ln1_b=ln1_b, ln2_g=ln2_g, ln2_b=ln2_b, w_in=mix_w_in, conv_w=hy_conv_w,
             conv_b=hy_conv_b, skip=hy_skip, q_norm=mla_q_norm, w_uq=mla_w_uq, kv_norm=mla_kv_norm,
             w_ukv=mla_w_ukv, w_out=mix_w_out, rw_mu=rw_mu, rw_w_r=rw_w_r, rw_w_k=rw_w_k, rw_w_v=rw_w_v,
             rw_w0=rw_w0, rw_w1=rw_w1, rw_w2=rw_w2, rw_a0=rw_a0, rw_a1=rw_a1, rw_a2=rw_a2, rw_g1=rw_g1,
             rw_g2=rw_g2, rw_k_k=rw_k_k, rw_k_a=rw_k_a, rw_r_k=rw_r_k, rw_gn_g=rw_gn_g, rw_gn_b=rw_gn_b,
             rw_w_o=rw_w_o, moe_w_group=moe_w_group, moe_w_expert=moe_w_expert,
             moe_w1=moe_w1.astype(BF16), moe_w3=moe_w3.astype(BF16), moe_w2=moe_w2.astype(BF16))
    outs = []
    for x in (x_prompt, x_sample):
        L = x.shape[1]
        tabs = _dft_tables(L)
        kspecs = [_hyena_filter_spectrum(L, hy_ffn_w1[i], hy_ffn_b1[i], hy_ffn_w2[i], hy_ffn_b2[i],
                                         hy_ffn_w3[i], hy_sin_freq[i], hy_decay[i], tabs)
                  for i in range((DEPTH + 1) // 2)]
        outs.append(_trunk(x, p, tabs, kspecs))
    return tuple(outs)
```

```python
import functools
import math

import jax
import jax.numpy as jnp
import numpy as np
from jax import lax
from jax.experimental import pallas as pl
from jax.experimental.pallas import tpu as pltpu

F32 = jnp.float32
BF16 = jnp.bfloat16

LANES = 128
VMEM_LIMIT_BYTES = 56 * 1024 * 1024

D_MODEL = 1024
DEPTH = 4
HY_W = D_MODEL // 2
HY_ORDER = 2
HY_BANDS = 16
MLA_HEADS = 8
MLA_NOPE = 64
MLA_ROPE = 32
MLA_V = 64
MLA_Q_LORA = 256
MLA_KV_LORA = 128
ROPE_THETA = 10000.0
RW_HEAD = 64
RW_HEADS = D_MODEL // RW_HEAD
RW_GN_EPS = 64e-5
MOE_GROUPS = 4
MOE_PER_GROUP = 8
MOE_EXPERTS = MOE_GROUPS * MOE_PER_GROUP
MOE_TOPK = 2
MOE_FF = 512
LN_EPS = 1e-5
RMS_EPS = 1e-6
DN_ALPHA = (2 * DEPTH) ** 0.25

RW_CHUNK = 64
HEADS_PER_SLAB = LANES // RW_HEAD


def _cparams(sem):
    return pltpu.CompilerParams(dimension_semantics=sem, vmem_limit_bytes=VMEM_LIMIT_BYTES)


def _run_staged(chains, skew):
    done = [False] * len(chains)
    rnd = 0
    while not all(done):
        for c, ch in enumerate(chains):
            if rnd >= skew * c and not done[c]:
                done[c] = not next(ch, False)
        rnd += 1


def _bdot(a, b):
    return jnp.dot(a.astype(BF16), b.astype(BF16), preferred_element_type=F32)


def _bdot_nt(a, b):
    return lax.dot_general(a.astype(BF16), b.astype(BF16), (((1,), (1,)), ((), ())),
                           preferred_element_type=F32)


def _bdot_tn(a, b):
    return lax.dot_general(a.astype(BF16), b.astype(BF16), (((0,), (0,)), ((), ())),
                           preferred_element_type=F32)


def _rwkv_scan_kernel(r_ref, lw_ref, k_ref, v_ref, kk_ref, b_ref, g_ref, rk_ref, gng_ref, gnb_ref,
                      o_ref, s_ref, *, tb):
    C = RW_CHUNK
    z = pl.program_id(0)
    fwd = z == 0

    @pl.when(pl.program_id(3) == 0)
    def _():
        s_ref[...] = jnp.zeros_like(s_ref)

    lane = lax.broadcasted_iota(jnp.int32, (1, LANES), 1)
    m0 = lane < RW_HEAD
    sgn = jnp.where(fwd, 1, -1)
    ri = lax.broadcasted_iota(jnp.int32, (C, C), 0)
    ci = lax.broadcasted_iota(jnp.int32, (C, C), 1)
    tri_c = jnp.where((ci - ri) * sgn <= 0, 1.0, 0.0).astype(BF16)
    r2 = lax.broadcasted_iota(jnp.int32, (2 * C, 2 * C), 0)
    c2 = lax.broadcasted_iota(jnp.int32, (2 * C, 2 * C), 1)
    head_r = jnp.where(r2 >= C, 1, 0)
    head_c = jnp.where(c2 >= C, 1, 0)
    lag = jnp.where(head_r == head_c, (c2 - r2) * sgn, 1)
    strict = lag < 0
    incl = lag <= 0

    def stack(x):
        return jnp.concatenate([jnp.where(m0, x, 0.0), jnp.where(m0, 0.0, x)], axis=0)

    def head_sum(x):
        s0 = jnp.sum(jnp.where(m0, x, 0.0), axis=-1, keepdims=True)
        s1 = jnp.sum(jnp.where(m0, 0.0, x), axis=-1, keepdims=True)
        return jnp.where(m0, s0, s1)

    def chunk(j, carry):
        start = jnp.where(fwd, j * C, tb - (j + 1) * C)
        start = pl.multiple_of(start, C)
        sl = pl.ds(start, C)
        _run_staged([slab_chunk(sl, slab) for slab in range(s_ref.shape[0])], skew=0)
        return carry

    def slab_chunk(sl, slab):
        ln = slice(slab * LANES, (slab + 1) * LANES)
        rk = rk_ref[:, ln]
        gng = gng_ref[:, ln]
        gnb = gnb_ref[:, ln]
        r = r_ref[sl, ln]
        lw = lw_ref[sl, ln]
        k = k_ref[sl, ln]
        v = v_ref[sl, ln]
        kk = kk_ref[sl, ln]
        b = b_ref[sl, ln]
        g = g_ref[sl, ln]
        S = s_ref[slab]

        lw_hi = lw.astype(BF16)
        rem = lw - lw_hi.astype(F32)
        lw_mid = rem.astype(BF16)
        lw_lo = (rem - lw_mid.astype(F32)).astype(BF16)
        cl3 = jnp.dot(tri_c, jnp.concatenate([lw_hi, lw_mid, lw_lo], axis=1), preferred_element_type=F32)
        cl = cl3[:, :LANES] + cl3[:, LANES:2 * LANES] + cl3[:, 2 * LANES:]
        yield True
        cl_last = jnp.where(fwd, cl[C - 1:C, :], cl[0:1, :])
        eneg = jnp.exp(-cl)
        epos = jnp.exp(cl)
        ehat = jnp.exp(cl_last - cl)
        at = -kk * jnp.exp(cl - lw)
        la_lr = jnp.concatenate([stack(at), stack(r * epos)], axis=0).astype(BF16)
        rb_rk = jnp.concatenate([stack(b * eneg), stack(k * eneg)], axis=0).astype(BF16)
        v2 = stack(v)

        sc = _bdot_nt(la_lr, rb_rk)
        hs = _bdot_nt(la_lr, S)
        ys = hs[2 * C:]
        yield True
        mab = jnp.where(strict, sc[:2 * C, :2 * C], 0.0)
        mak = jnp.where(strict, sc[:2 * C, 2 * C:], 0.0)
        nrb = jnp.where(incl, sc[2 * C:, :2 * C], 0.0)
        nrk = jnp.where(incl, sc[2 * C:, 2 * C:], 0.0)
        x = hs[:2 * C] + _bdot(mak, v2)
        mp = mab
        yield True
        x = x + _bdot(mp, x)
        for _ in range(int(math.log2(C)) - 1):
            mp = _bdot(mp, mp)
            yield True
            x = x + _bdot(mp, x)
        yield True
        uv = jnp.concatenate([x, v2], axis=0)
        y2 = ys + _bdot(jnp.concatenate([nrb, nrk], axis=1), uv)
        bh_kh = jnp.concatenate([stack(b * ehat), stack(k * ehat)], axis=0)
        s_ref[slab] = S * jnp.exp(cl_last) + _bdot_tn(uv, bh_kh)
        yield True
        y = y2[:C] + y2[C:]

        mean = head_sum(y) * (1.0 / RW_HEAD)
        d = y - mean
        var = head_sum(d * d) * (1.0 / RW_HEAD)
        yn = d * lax.rsqrt(var + RW_GN_EPS) * gng + gnb
        bonus = head_sum(r * k * rk) * v
        o_ref[sl, ln] = (yn + bonus) * g

    lax.fori_loop(0, tb // C, chunk, 0)


RW_SLABS_PER_STEP = 8


def _rwkv_scan(r, lw, k, v, kk, b, g, r_k, gn_g, gn_b, *, tb=512):
    Z, B, L, D = r.shape
    tb = min(tb, L)
    nblk = L // tb
    ns = min(RW_SLABS_PER_STEP, D // LANES)
    width = ns * LANES

    def amap(z, bb, h, i):
        return (z, bb, jnp.where(z == 0, i, nblk - 1 - i), h)

    act = pl.BlockSpec((None, None, tb, width), amap)
    par = pl.BlockSpec((1, width), lambda z, bb, h, i: (0, h))
    return pl.pallas_call(
        functools.partial(_rwkv_scan_kernel, tb=tb),
        grid=(Z, B, D // width, nblk),
        in_specs=[act] * 7 + [par] * 3,
        out_specs=act,
        out_shape=jax.ShapeDtypeStruct((Z, B, L, D), F32),
        scratch_shapes=[pltpu.VMEM((ns, LANES, LANES), F32)],
        compiler_params=_cparams(("parallel", "parallel", "parallel", "arbitrary")),
        name="rwkv_scan",
    )(r, lw, k, v, kk, b, g, r_k.reshape(1, D), gn_g.reshape(1, D), gn_b.reshape(1, D))


HALO_ROWS = 8


def _halo_specs(tm, width, seq_len, lead):
    per = tm // HALO_ROWS
    last = seq_len // HALO_ROWS - 1

    def main(*g):
        return (g[lead], g[lead + 1], 0)

    def prev(*g):
        return (g[lead], jnp.maximum(g[lead + 1] * per - 1, 0), 0)

    def nxt(*g):
        return (g[lead], jnp.minimum((g[lead + 1] + 1) * per, last), 0)

    return (pl.BlockSpec((None, tm, width), main), pl.BlockSpec((None, HALO_ROWS, width), prev),
            pl.BlockSpec((None, HALO_ROWS, width), nxt))


def _shifted(x, prev_ref, next_ref, i, n_tiles):
    tm = x.shape[0]
    prev_row = jnp.where(i == 0, 0.0, prev_ref[HALO_ROWS - 1:HALO_ROWS, :])
    next_row = jnp.where(i == n_tiles - 1, 0.0, next_ref[0:1, :])
    row = lax.broadcasted_iota(jnp.int32, (tm, 1), 0)
    x_prev = jnp.where(row == 0, prev_row, pltpu.roll(x, 1, axis=0))
    x_next = jnp.where(row == tm - 1, next_row, pltpu.roll(x, tm - 1, axis=0))
    return x_prev, x_next


def _layer_norm_rows(y, g, b):
    mu = jnp.mean(y, axis=-1, keepdims=True)
    d = y - mu
    var = jnp.mean(d * d, axis=-1, keepdims=True)
    return d * lax.rsqrt(var + LN_EPS) * g + b


RW_LORA_PAD = 128
RW_GATE_PAD = 256
RW_DECAY_SCALE = math.exp(-0.5)


def _rwkv_proj_kernel(x_ref, xp_ref, xn_ref, mu_ref, wr_ref, wk_ref, wv_ref, w0_ref, w1_ref, w2_ref,
                      a0_ref, a1_ref, a2_ref, g1_ref, g2_ref, kk_ref, ka_ref,
                      r_out, lw_out, k_out, v_out, kk_out, b_out, g_out):
    z = pl.program_id(0)
    i = pl.program_id(2)
    x = x_ref[...]
    x_prev, x_next = _shifted(x, xp_ref, xn_ref, i, pl.num_programs(2))
    xx = jnp.where(z == 0, x_prev, x_next) - x
    mu = mu_ref[...]

    def mix(j):
        return (x + xx * mu[j:j + 1, :]).astype(BF16)

    r = jnp.dot(mix(0), wr_ref[...], preferred_element_type=F32)
    k = jnp.dot(mix(2), wk_ref[...], preferred_element_type=F32)
    v = jnp.dot(mix(3), wv_ref[...], preferred_element_type=F32)
    lw = w0_ref[...] + _bdot(jnp.tanh(jnp.dot(mix(1), w1_ref[...], preferred_element_type=F32)), w2_ref[...])
    logw = -jax.nn.sigmoid(lw) * RW_DECAY_SCALE
    a = jax.nn.sigmoid(a0_ref[...] + _bdot(jnp.dot(mix(4), a1_ref[...], preferred_element_type=F32), a2_ref[...]))
    g = _bdot(jax.nn.sigmoid(jnp.dot(mix(5), g1_ref[...], preferred_element_type=F32)), g2_ref[...])

    kkr = k * kk_ref[...]
    lane = lax.broadcasted_iota(jnp.int32, (1, LANES), 1)
    m0 = lane < RW_HEAD
    slabs = []
    for s in range(x.shape[1] // LANES):
        q = kkr[:, s * LANES:(s + 1) * LANES]
        q2 = q * q
        n0 = jnp.sum(jnp.where(m0, q2, 0.0), axis=-1, keepdims=True)
        n1 = jnp.sum(jnp.where(m0, 0.0, q2), axis=-1, keepdims=True)
        nrm = jnp.maximum(jnp.sqrt(jnp.where(m0, n0, n1)), 1e-12)
        slabs.append(q / nrm)
    kk = jnp.concatenate(slabs, axis=1)

    r_out[...] = r
    lw_out[...] = logw
    k_out[...] = k * (1.0 + (a - 1.0) * ka_ref[...])
    v_out[...] = v
    kk_out[...] = kk
    b_out[...] = kk * a
    g_out[...] = g


def _pad_axis(w, axis, size):
    pad = [(0, 0)] * w.ndim
    pad[axis] = (0, size - w.shape[axis])
    return jnp.pad(w, pad)


def _rwkv_proj(x, p, *, tm=512):
    B, L, D = x.shape
    tm = min(tm, L)
    main, prev, nxt = _halo_specs(tm, D, L, lead=1)
    full2 = lambda shape: pl.BlockSpec(shape, lambda z, b, i: (0, 0))
    perz = lambda shape: pl.BlockSpec((None,) + shape, lambda z, b, i: (z, 0, 0))
    out_spec = pl.BlockSpec((None, None, tm, D), lambda z, b, i: (z, b, i, 0))
    w1 = _pad_axis(p['w1'], 2, RW_LORA_PAD).astype(BF16)
    w2 = _pad_axis(p['w2'], 1, RW_LORA_PAD).astype(BF16)
    a1 = _pad_axis(p['a1'], 2, RW_LORA_PAD).astype(BF16)
    a2 = _pad_axis(p['a2'], 1, RW_LORA_PAD).astype(BF16)
    g1 = _pad_axis(p['g1'], 2, RW_GATE_PAD).astype(BF16)
    g2 = _pad_axis(p['g2'], 1, RW_GATE_PAD).astype(BF16)
    outs = pl.pallas_call(
        _rwkv_proj_kernel,
        grid=(2, B, L // tm),
        in_specs=[main, prev, nxt, perz((6, D)), full2((D, D)), full2((D, D)), full2((D, D)),
                  perz((1, D)), perz((D, RW_LORA_PAD)), perz((RW_LORA_PAD, D)),
                  perz((1, D)), perz((D, RW_LORA_PAD)), perz((RW_LORA_PAD, D)),
                  perz((D, RW_GATE_PAD)), perz((RW_GATE_PAD, D)), full2((1, D)), full2((1, D))],
        out_specs=[out_spec] * 7,
        out_shape=[jax.ShapeDtypeStruct((2, B, L, D), F32)] * 7,
        compiler_params=_cparams(("parallel", "parallel", "parallel")),
        name="rwkv_proj",
    )(x, x, x, p['mu'], p['w_r'].astype(BF16), p['w_k'].astype(BF16), p['w_v'].astype(BF16),
      p['w0'].reshape(2, 1, D), w1, w2, p['a0'].reshape(2, 1, D), a1, a2, g1, g2,
      p['k_k'].reshape(1, D), p['k_a'].reshape(1, D))
    return outs


def _rw_out_ln_kernel(o_ref, w_ref, res_ref, g_ref, b_ref, out_ref):
    s = (o_ref[0] + o_ref[1]).astype(BF16)
    m = jnp.dot(s, w_ref[...], preferred_element_type=F32)
    out_ref[...] = _layer_norm_rows(DN_ALPHA * res_ref[...] + m, g_ref[...], b_ref[...])


def _rw_out_ln(o, w_o, res, ln_g, ln_b, *, tm=512):
    _, T, D = o.shape
    tm = min(tm, T)
    row = pl.BlockSpec((tm, D), lambda i: (i, 0))
    vec = pl.BlockSpec((1, D), lambda i: (0, 0))
    return pl.pallas_call(
        _rw_out_ln_kernel,
        grid=(T // tm,),
        in_specs=[pl.BlockSpec((2, tm, D), lambda i: (0, i, 0)), pl.BlockSpec((D, D), lambda i: (0, 0)),
                  row, vec, vec],
        out_specs=row,
        out_shape=jax.ShapeDtypeStruct((T, D), F32),
        compiler_params=_cparams(("parallel",)),
        name="rw_out_ln",
    )(o, w_o.astype(BF16), res, ln_g.reshape(1, D), ln_b.reshape(1, D))


def _rwkv_layer(x, p, ln_g, ln_b):
    B, L, D = x.shape
    r, lw, k, v, kk, b, g = _rwkv_proj(x, p)
    o = _rwkv_scan(r, lw, k, v, kk, b, g, p['r_k'].reshape(D), p['gn_g'], p['gn_b'])
    return _rw_out_ln(o.reshape(2, B * L, D), p['w_o'], x.reshape(B * L, D), ln_g, ln_b).reshape(B, L, D)


HY_U_W = (HY_ORDER + 1) * HY_W
MLA_HEAD_PAD = LANES
MLA_QK_W = MLA_HEADS * MLA_HEAD_PAD
MLA_V_W = MLA_HEADS * MLA_V
MLA_SCALE = (MLA_NOPE + MLA_ROPE) ** -0.5 * math.log2(math.e)
HALF_ROPE = MLA_ROPE // 2


def _rms_rows(c, g):
    return c * lax.rsqrt(jnp.mean(c * c, axis=-1, keepdims=True) + RMS_EPS) * g


def _in_proj_kernel(x_ref, wu_ref, ws_ref, wq_ref, wkv_ref, qn_ref, kvn_ref, cos_ref, sin_ref,
                    u_out, q_out, k_out, v_out):
    xb = x_ref[...].astype(BF16)
    u_out[...] = jnp.dot(xb, wu_ref[...], preferred_element_type=F32)
    small = jnp.dot(xb, ws_ref[...], preferred_element_type=F32)
    c_q = small[:, :MLA_Q_LORA]
    c_kv = small[:, MLA_Q_LORA:MLA_Q_LORA + MLA_KV_LORA]
    kpe_a = small[:, MLA_Q_LORA + MLA_KV_LORA:MLA_Q_LORA + MLA_KV_LORA + LANES]
    kpe_b = small[:, MLA_Q_LORA + MLA_KV_LORA + LANES:]
    cos = cos_ref[...]
    sin = sin_ref[...]
    cos_h = jnp.concatenate([cos] * MLA_HEADS, axis=1)
    sin_h = jnp.concatenate([sin] * MLA_HEADS, axis=1)

    qq = _bdot(_rms_rows(c_q, qn_ref[...]), wq_ref[...])
    q_out[...] = (qq[:, :MLA_QK_W] * cos_h + qq[:, MLA_QK_W:] * sin_h).astype(BF16)

    kv = _bdot(_rms_rows(c_kv, kvn_ref[...]), wkv_ref[...])
    kpe = kpe_a * cos + kpe_b * sin
    k_out[...] = (kv[:, :MLA_QK_W] + jnp.concatenate([kpe] * MLA_HEADS, axis=1)).astype(BF16)
    v_out[...] = kv[:, MLA_QK_W:].T.astype(BF16)


def _rope_head_tables(L):
    inv = ROPE_THETA ** (-jnp.arange(0, MLA_ROPE, 2, dtype=F32) / MLA_ROPE)
    ang = jnp.arange(L, dtype=F32)[:, None] * inv[None, :]
    cos, sin = jnp.cos(ang), jnp.sin(ang)
    ones = jnp.ones((L, MLA_NOPE), F32)
    zn = jnp.zeros((L, MLA_NOPE), F32)
    zp = jnp.zeros((L, MLA_HEAD_PAD - MLA_NOPE - MLA_ROPE), F32)
    return (jnp.concatenate([ones, cos, cos, zp], axis=1), jnp.concatenate([zn, sin, sin, zp], axis=1))


def _mla_weights(w_in, w_uq, w_ukv):
    s1 = HY_U_W
    s2 = s1 + MLA_Q_LORA
    s3 = s2 + MLA_KV_LORA
    kx1, kx2 = w_in[:, s3:s3 + HALF_ROPE], w_in[:, s3 + HALF_ROPE:]
    zl = jnp.zeros((w_in.shape[0], MLA_NOPE), F32)
    zr = jnp.zeros((w_in.shape[0], MLA_HEAD_PAD - MLA_NOPE - MLA_ROPE), F32)
    kpe_a = jnp.concatenate([zl, kx1, kx2, zr], axis=1)
    kpe_b = jnp.concatenate([zl, -kx2, kx1, zr], axis=1)
    w_small = jnp.concatenate([w_in[:, s1:s3], kpe_a, kpe_b], axis=1)
    q3 = w_uq.reshape(MLA_Q_LORA, MLA_HEADS, MLA_NOPE + MLA_ROPE) * MLA_SCALE
    nope, x1, x2 = q3[..., :MLA_NOPE], q3[..., MLA_NOPE:MLA_NOPE + HALF_ROPE], q3[..., MLA_NOPE + HALF_ROPE:]
    zq = jnp.zeros((MLA_Q_LORA, MLA_HEADS, MLA_HEAD_PAD - MLA_NOPE - MLA_ROPE), F32)
    wqa = jnp.concatenate([nope, x1, x2, zq], axis=-1).reshape(MLA_Q_LORA, MLA_QK_W)
    wqb = jnp.concatenate([jnp.zeros_like(nope), -x2, x1, zq], axis=-1).reshape(MLA_Q_LORA, MLA_QK_W)
    kv3 = w_ukv.reshape(MLA_KV_LORA, MLA_HEADS, MLA_NOPE + MLA_V)
    wkn = jnp.concatenate([kv3[..., :MLA_NOPE], jnp.zeros((MLA_KV_LORA, MLA_HEADS, MLA_HEAD_PAD - MLA_NOPE), F32)],
                          axis=-1).reshape(MLA_KV_LORA, MLA_QK_W)
    wv = kv3[..., MLA_NOPE:].reshape(MLA_KV_LORA, MLA_V_W)
    return (w_in[:, :s1].astype(BF16), w_small.astype(BF16), jnp.concatenate([wqa, wqb], axis=1).astype(BF16),
            jnp.concatenate([wkn, wv], axis=1).astype(BF16))


def _in_proj(x, w_in, w_uq, w_ukv, q_norm, kv_norm, *, tm=256):
    B, L, D = x.shape
    tm = min(tm, L)
    wu, ws, wq, wkv = _mla_weights(w_in, w_uq, w_ukv)
    cos, sin = _rope_head_tables(L)
    row = lambda w: pl.BlockSpec((None, tm, w), lambda b, i: (b, i, 0))
    full = lambda a: pl.BlockSpec(a.shape, lambda b, i: (0, 0))
    pos = pl.BlockSpec((tm, LANES), lambda b, i: (i, 0))
    qn = q_norm.reshape(1, -1)
    kvn = kv_norm.reshape(1, -1)
    return pl.pallas_call(
        _in_proj_kernel,
        grid=(B, L // tm),
        in_specs=[row(D), full(wu), full(ws), full(wq), full(wkv), full(qn), full(kvn), pos, pos],
        out_specs=[row(HY_U_W), row(MLA_QK_W), row(MLA_QK_W),
                   pl.BlockSpec((None, MLA_V_W, tm), lambda b, i: (b, 0, i))],
        out_shape=[jax.ShapeDtypeStruct((B, L, HY_U_W), F32), jax.ShapeDtypeStruct((B, L, MLA_QK_W), BF16),
                   jax.ShapeDtypeStruct((B, L, MLA_QK_W), BF16), jax.ShapeDtypeStruct((B, MLA_V_W, L), BF16)],
        compiler_params=_cparams(("parallel", "parallel")),
        name="in_proj",
    )(x, wu, ws, wq, wkv, qn, kvn, cos, sin)


MLA_PAIR = 2
ATTN_SUM_ROWS = 8


def _attn_kernel(q_ref, k_ref, vt_ref, o_ref, s_ref, p_ref, mx_ref, al_ref, m_ref, l_ref, acc_ref, *, tk):
    tq = q_ref.shape[0]
    nk = k_ref.shape[0] // tk
    heads = range(MLA_PAIR)
    head_lanes = lambda h: slice(h * MLA_HEAD_PAD, (h + 1) * MLA_HEAD_PAD)
    head_rows = lambda h: slice(h * MLA_V, (h + 1) * MLA_V)

    m_ref[...] = jnp.full_like(m_ref, -jnp.inf)
    l_ref[...] = jnp.zeros_like(l_ref)
    acc_ref[...] = jnp.zeros_like(acc_ref)

    ones = jnp.ones((ATTN_SUM_ROWS, tk), BF16)

    p_ref[1] = jnp.zeros_like(p_ref[1])
    al_ref[1] = jnp.ones_like(al_ref[1])

    def scores(j, buf):
        keys = pl.ds(pl.multiple_of(j * tk, tk), tk)
        for h in heads:
            st = lax.dot_general(k_ref[keys, head_lanes(h)], q_ref[:, head_lanes(h)],
                                 (((1,), (1,)), ((), ())), preferred_element_type=F32)
            s_ref[buf, h] = st
            mx_ref[buf, h] = jnp.max(st, axis=0, keepdims=True)

    def softmax(buf):
        for h in heads:
            m_old = m_ref[h]
            m_new = jnp.maximum(m_old, mx_ref[buf, h])
            al_ref[buf, h] = jnp.exp2(m_old - m_new)
            m_ref[h] = m_new
            p_ref[buf, h] = jnp.exp2(s_ref[buf, h] - m_new).astype(BF16)

    def values(j, buf):
        keys = pl.ds(pl.multiple_of(j * tk, tk), tk)
        for h in heads:
            p = p_ref[buf, h]
            alpha = al_ref[buf, h]
            pv = jnp.dot(vt_ref[head_rows(h), keys], p, preferred_element_type=F32)
            acc_ref[head_rows(h), :] = alpha * acc_ref[head_rows(h), :] + pv
            psum = jnp.dot(ones, p, preferred_element_type=F32)
            l_ref[h] = alpha * l_ref[h] + psum[0:1, :]

    scores(0, 0)

    def pair(i, carry):
        for u in range(0, ATTN_UNROLL, 2):
            j = ATTN_UNROLL * i + u
            softmax(0)
            scores(j + 1, 1)
            values(jnp.maximum(j - 1, 0), 1)
            softmax(1)
            scores(jnp.minimum(j + 2, nk - 1), 0)
            values(j, 0)
        return carry

    lax.fori_loop(0, nk // ATTN_UNROLL, pair, 0)
    values(nk - 1, 1)

    inv = jnp.concatenate([jnp.broadcast_to(1.0 / l_ref[h], (MLA_V, tq)) for h in heads], axis=0)
    o_ref[...] = (acc_ref[...] * inv).T


ATTN_UNROLL = 2


def _attention(q, k, vt, *, tq=512, tk=1024):
    B, L, _ = q.shape
    tq, tk = min(tq, L), min(tk, L // ATTN_UNROLL)
    pairs = MLA_HEADS // MLA_PAIR
    return pl.pallas_call(
        functools.partial(_attn_kernel, tk=tk),
        grid=(B, pairs, L // tq),
        in_specs=[pl.BlockSpec((None, tq, MLA_PAIR * MLA_HEAD_PAD), lambda b, h, qi: (b, qi, h)),
                  pl.BlockSpec((None, L, MLA_PAIR * MLA_HEAD_PAD), lambda b, h, qi: (b, 0, h)),
                  pl.BlockSpec((None, LANES, L), lambda b, h, qi: (b, h, 0))],
        out_specs=pl.BlockSpec((None, tq, LANES), lambda b, h, qi: (b, qi, h)),
        out_shape=jax.ShapeDtypeStruct((B, L, MLA_V_W), F32),
        scratch_shapes=[pltpu.VMEM((2, MLA_PAIR, tk, tq), F32), pltpu.VMEM((2, MLA_PAIR, tk, tq), BF16),
                        pltpu.VMEM((2, MLA_PAIR, 1, tq), F32), pltpu.VMEM((2, MLA_PAIR, 1, tq), F32),
                        pltpu.VMEM((MLA_PAIR, 1, tq), F32),
                        pltpu.VMEM((MLA_PAIR, 1, tq), F32), pltpu.VMEM((LANES, tq), F32)],
        compiler_params=_cparams(("parallel", "parallel", "parallel")),
        name="attention",
    )(q, k, vt)


def _split3_rows(x):
    hi = x.astype(BF16)
    lo = (x - hi.astype(F32)).astype(BF16)
    return jnp.concatenate([hi, lo, hi], axis=0)


def _split3_cols(m):
    hi = m.astype(BF16)
    lo = (m - hi.astype(F32)).astype(BF16)
    return jnp.concatenate([hi, hi, lo], axis=-1)


def _fft_factor(n):
    n1 = 1 << ((n.bit_length() - 1 + 1) // 2)
    return n1, n // n1


def _dft_tables(L):
    N = 2 * L
    N1, N2 = _fft_factor(N)
    h = N1 // 2
    k1 = jnp.arange(N1, dtype=jnp.int32)
    ang1 = (-2.0 * math.pi / N1) * ((k1[:, None] * k1[None, :]) % N1).astype(F32)
    f1r, f1i = jnp.cos(ang1), jnp.sin(ang1)
    lead_pair = jnp.concatenate([jnp.concatenate([f1r[:, :h], -f1i[:, :h]], axis=1),
                                 jnp.concatenate([f1i[:, :h], f1r[:, :h]], axis=1)], axis=0)
    m1 = jnp.arange(h, dtype=jnp.int32)
    cols_gen, cols_first = N1 - 1 - m1, (N1 - m1) % N1
    has_lag = (m1 > 0).astype(F32)[None, :]
    filt_fwd = jnp.concatenate([f1r[:, :h], f1i[:, :h]], axis=0)
    filt_bwd = jnp.stack([jnp.concatenate([f1r[:, cols_gen], f1i[:, cols_gen]], axis=0),
                          jnp.concatenate([f1r[:, cols_first] * has_lag, f1i[:, cols_first] * has_lag], axis=0)])
    f1rt, f1it = f1r.T[:h], f1i.T[:h]
    lead_inv = jnp.concatenate([jnp.concatenate([f1rt, f1it], axis=1),
                                jnp.concatenate([-f1it, f1rt], axis=1)], axis=0) * (1.0 / N)
    n2 = jnp.arange(N2, dtype=jnp.int32)
    freq = k1[:, None, None] + N1 * n2[None, :, None]
    idx = (freq * n2[None, None, :]) % N
    ang2 = (-2.0 * math.pi / N) * idx.astype(F32)
    gr, gi = jnp.cos(ang2), jnp.sin(ang2)
    inner_fwd = jnp.concatenate([jnp.concatenate([gr, -gi], axis=2),
                                 jnp.concatenate([gi, gr], axis=2)], axis=1)
    grt, git = jnp.swapaxes(gr, 1, 2), jnp.swapaxes(gi, 1, 2)
    inner_inv = jnp.concatenate([jnp.concatenate([grt, git], axis=2),
                                 jnp.concatenate([-git, grt], axis=2)], axis=1)
    return dict(N1=N1, N2=N2, lead_pair=_split3_cols(lead_pair), filt_fwd=_split3_cols(filt_fwd),
                filt_bwd=_split3_cols(filt_bwd), lead_inv=_split3_cols(lead_inv), inner_fwd=_split3_cols(inner_fwd),
                inner_inv=_split3_cols(inner_inv))


FFT_LANE_TILE = 1024


def _dft_lead_kernel(m_ref, x_ref, o_ref):
    k = m_ref.shape[1] // 3
    x = x_ref[...].reshape(k, x_ref.shape[-1])
    o_ref[...] = jnp.dot(m_ref[...], _split3_rows(x), preferred_element_type=F32).reshape(o_ref.shape)


def _dft_lead_gate_kernel(m_ref, x_ref, z_ref, g_ref, sk_ref, o_ref):
    k = m_ref.shape[1] // 3
    x = x_ref[...].reshape(k, x_ref.shape[-1])
    conv = jnp.dot(m_ref[...], _split3_rows(x), preferred_element_type=F32).reshape(o_ref.shape)
    o_ref[...] = g_ref[...] * (conv + sk_ref[...] * z_ref[...])


def _dft_inner_conv_kernel(mf_ref, mi_ref, a_ref, kf_ref, o_ref):
    n2 = a_ref.shape[1]
    a = a_ref[...].reshape(2 * n2, a_ref.shape[-1])
    x = jnp.dot(mf_ref[...], _split3_rows(a), preferred_element_type=F32)
    xr, xi = x[:n2], x[n2:]
    kr, ki = kf_ref[0], kf_ref[1]
    y = jnp.concatenate([xr * kr - xi * ki, xr * ki + xi * kr], axis=0)
    o_ref[...] = jnp.dot(mi_ref[...], _split3_rows(y), preferred_element_type=F32).reshape(o_ref.shape)


def _dft_inner_filter_kernel(mf_ref, a_ref, s_ref, o_ref):
    n2 = a_ref.shape[1]
    a = a_ref[...].reshape(2 * n2, a_ref.shape[-1])
    x = jnp.dot(mf_ref[...], _split3_rows(a), preferred_element_type=F32) * s_ref[...]
    o_ref[...] = x.reshape(o_ref.shape)


def _dft_lead_filter_kernel(mf_ref, mb_ref, fwd_ref, bwd_ref, o_ref):
    first = jnp.where(pl.program_id(0) == 0, 1, 0)
    a = (jnp.dot(mf_ref[...], _split3_rows(fwd_ref[...]), preferred_element_type=F32)
         + jnp.dot(mb_ref[first], _split3_rows(bwd_ref[...]), preferred_element_type=F32))
    o_ref[...] = a.reshape(o_ref.shape)


def _filter_spectrum(hf, inv_norm, tabs):
    L, Cf = hf.shape[0], hf.shape[1] // 2
    N1, N2 = tabs['N1'], tabs['N2']
    h = N1 // 2
    hv = hf.reshape(h, N2 * 2 * Cf)
    a = pl.pallas_call(
        _dft_lead_filter_kernel,
        grid=(N2,),
        in_specs=[pl.BlockSpec((2 * N1, 3 * h), lambda j: (0, 0)), pl.BlockSpec((2, 2 * N1, 3 * h), lambda j: (0, 0, 0)),
                  pl.BlockSpec((h, Cf), lambda j: (0, 2 * j)),
                  pl.BlockSpec((h, Cf), lambda j: (0, 2 * lax.rem(N2 - j, N2) + 1))],
        out_specs=pl.BlockSpec((2, N1, Cf), lambda j: (0, 0, j)),
        out_shape=jax.ShapeDtypeStruct((2, N1, N2 * Cf), F32),
        compiler_params=_cparams(("arbitrary",)),
        name="hyena_filter_dft_lead",
    )(tabs['filt_fwd'], tabs['filt_bwd'], hv, hv)
    return pl.pallas_call(
        _dft_inner_filter_kernel,
        grid=(N1,),
        in_specs=[pl.BlockSpec((None, 2 * N2, 6 * N2), lambda k1: (k1, 0, 0)),
                  pl.BlockSpec((2, None, N2, Cf), lambda k1: (0, k1, 0, 0)),
                  pl.BlockSpec((1, Cf), lambda k1: (0, 0))],
        out_specs=pl.BlockSpec((2, None, N2, Cf), lambda k1: (0, k1, 0, 0)),
        out_shape=jax.ShapeDtypeStruct((2, N1, N2, Cf), F32),
        compiler_params=_cparams(("parallel",)),
        name="hyena_filter_dft_inner",
    )(tabs['inner_fwd'], a.reshape(2, N1, N2, Cf), inv_norm)


def _long_conv_gate(z, gate, skip, kspec, order, tabs):
    B, L, C = z.shape
    N1, N2 = tabs['N1'], tabs['N2']
    h = N1 // 2
    P = B // 2
    W = N2 * C
    wc = min(FFT_LANE_TILE, W)
    zv = z.reshape(B, h, W)
    a = pl.pallas_call(
        _dft_lead_kernel,
        grid=(P, W // wc),
        in_specs=[pl.BlockSpec((2 * N1, 3 * N1), lambda p, j: (0, 0)),
                  pl.BlockSpec((2, h, wc), lambda p, j: (p, 0, j))],
        out_specs=pl.BlockSpec((None, 2, N1, wc), lambda p, j: (p, 0, 0, j)),
        out_shape=jax.ShapeDtypeStruct((P, 2, N1, W), F32),
        compiler_params=_cparams(("parallel", "parallel")),
        name="hyena_dft_lead",
    )(tabs['lead_pair'], zv)
    inner = pl.BlockSpec((None, 2, None, N2, C), lambda p, k1: (p, 0, k1, 0, 0))
    mat = pl.BlockSpec((None, 2 * N2, 6 * N2), lambda p, k1: (k1, 0, 0))
    bm = pl.pallas_call(
        _dft_inner_conv_kernel,
        grid=(P, N1),
        in_specs=[mat, mat, inner, pl.BlockSpec((2, None, N2, C), lambda p, k1: (0, k1, 0, order))],
        out_specs=inner,
        out_shape=jax.ShapeDtypeStruct((P, 2, N1, N2, C), F32),
        compiler_params=_cparams(("parallel", "parallel")),
        name="hyena_dft_inner",
    )(tabs['inner_fwd'], tabs['inner_inv'], a.reshape(P, 2, N1, N2, C), kspec)
    pair = pl.BlockSpec((2, h, wc), lambda p, j: (p, 0, j))
    sk = jnp.tile(skip.reshape(1, C), (1, wc // C))
    out = pl.pallas_call(
        _dft_lead_gate_kernel,
        grid=(P, W // wc),
        in_specs=[pl.BlockSpec((N1, 6 * N1), lambda p, j: (0, 0)),
                  pl.BlockSpec((None, 2, N1, wc), lambda p, j: (p, 0, 0, j)), pair, pair,
                  pl.BlockSpec((1, wc), lambda p, j: (0, 0))],
        out_specs=pair,
        out_shape=jax.ShapeDtypeStruct((B, h, W), F32),
        compiler_params=_cparams(("parallel", "parallel")),
        name="hyena_dft_lead_inv",
    )(tabs['lead_inv'], bm.reshape(P, 2, N1, W), zv, gate.reshape(B, h, W), sk)
    return out.reshape(B, L, C)


def _short_conv_kernel(u_ref, up_ref, un_ref, cw_ref, cb_ref, z_out, x1_out, x2_out):
    i = pl.program_id(1)
    u = u_ref[...]
    u_prev, u_next = _shifted(u, up_ref, un_ref, i, pl.num_programs(1))
    cw = cw_ref[...]
    y = u_prev * cw[0:1, :] + u * cw[1:2, :] + u_next * cw[2:3, :] + cb_ref[...]
    z_out[...] = y[:, :HY_W]
    x1_out[...] = y[:, HY_W:2 * HY_W]
    x2_out[...] = y[:, 2 * HY_W:]


def _short_conv(u, conv_w, conv_b, *, tm=512):
    B, L, W = u.shape
    tm = min(tm, L)
    main, prev, nxt = _halo_specs(tm, W, L, lead=0)
    part = pl.BlockSpec((None, tm, HY_W), lambda b, i: (b, i, 0))
    return pl.pallas_call(
        _short_conv_kernel,
        grid=(B, L // tm),
        in_specs=[main, prev, nxt, pl.BlockSpec((3, W), lambda b, i: (0, 0)), pl.BlockSpec((1, W), lambda b, i: (0, 0))],
        out_specs=[part] * 3,
        out_shape=[jax.ShapeDtypeStruct((B, L, HY_W), F32)] * 3,
        compiler_params=_cparams(("parallel", "parallel")),
        name="hyena_short_conv",
    )(u, u, u, conv_w, conv_b.reshape(1, W))


HY_FILT_HID = 64
HY_FILT_OUT = 2 * HY_ORDER * HY_W
TWO_PI = 2.0 * math.pi


def _hdot(a, b):
    return jnp.dot(a, b, precision=lax.Precision.HIGHEST, preferred_element_type=F32)


def _filter_mlp_kernel(bands_ref, w1t_ref, w1s_ref, w1c_ref, b1_ref, w2_ref, b2_ref, w3_ref, fr_ref, dec_ref,
                       h_out, abs_out, *, seq_len):
    i = pl.program_id(0)
    tm = h_out.shape[0]
    row = i * tm + lax.broadcasted_iota(jnp.int32, (tm, 1), 0)
    t = row.astype(F32) / seq_len
    ph = (TWO_PI * t) * bands_ref[...]
    fr = fr_ref[...]
    pre1 = t * w1t_ref[...] + _hdot(jnp.sin(ph), w1s_ref[...]) + _hdot(jnp.cos(ph), w1c_ref[...]) + b1_ref[...]
    h1 = jnp.sin(fr * pre1)
    h2 = jnp.sin(fr * (_hdot(h1, w2_ref[...]) + b2_ref[...]))
    hf = _hdot(h2, w3_ref[...]) * jnp.exp(-t * jnp.abs(dec_ref[...]))
    h_out[...] = hf

    @pl.when(i == 0)
    def _():
        abs_out[...] = jnp.zeros_like(abs_out)

    col = lax.broadcasted_iota(jnp.int32, (1, HY_FILT_OUT), 1)
    used = jnp.where(jnp.logical_and(row == 0, col >= HY_FILT_OUT // 2), 0.0, jnp.abs(hf))
    abs_out[...] += jnp.sum(used, axis=0, keepdims=True)


def _hyena_filter_spectrum(L, w1, b1, w2, b2, w3, freq, decay, tabs, *, tm=256):
    tm = min(tm, L)
    bands = _pad_axis(jnp.linspace(1e-4, HY_BANDS - 1, HY_BANDS, dtype=F32).reshape(1, HY_BANDS), 1, LANES)
    w1p = _pad_axis(w1, 1, LANES)
    w1t = w1p[0:1]
    w1s = _pad_axis(w1p[1:1 + HY_BANDS], 0, LANES)
    w1c = _pad_axis(w1p[1 + HY_BANDS:], 0, LANES)
    pad_vec = lambda v: _pad_axis(v.reshape(1, -1), 1, LANES)
    w2p = _pad_axis(_pad_axis(w2, 0, LANES), 1, LANES)
    w3p = _pad_axis(w3, 0, LANES)
    full = lambda a: pl.BlockSpec(a.shape, lambda i: (0, 0))
    args = (bands, w1t, w1s, w1c, pad_vec(b1), w2p, pad_vec(b2), w3p, pad_vec(freq), decay.reshape(1, HY_FILT_OUT))
    hf, abs_sum = pl.pallas_call(
        functools.partial(_filter_mlp_kernel, seq_len=L),
        grid=(L // tm,),
        in_specs=[full(a) for a in args],
        out_specs=[pl.BlockSpec((tm, HY_FILT_OUT), lambda i: (i, 0)), pl.BlockSpec((1, HY_FILT_OUT), lambda i: (0, 0))],
        out_shape=[jax.ShapeDtypeStruct((L, HY_FILT_OUT), F32), jax.ShapeDtypeStruct((1, HY_FILT_OUT), F32)],
        compiler_params=_cparams(("arbitrary",)),
        name="hyena_filter_mlp",
    )(*args)
    half = HY_FILT_OUT // 2
    inv_norm = 1.0 / (abs_sum[:, :half] + abs_sum[:, half:] + 1e-6)
    return _filter_spectrum(hf, inv_norm, tabs)


def _hyena(u, conv_w, conv_b, skip, kspec, tabs):
    z, x1, x2 = _short_conv(u, conv_w, conv_b)
    z = _long_conv_gate(z, x1, skip[0], kspec, 0, tabs)
    return _long_conv_gate(z, x2, skip[1], kspec, 1, tabs)


def _mix_out_ln_kernel(a_ref, b_ref, wa_ref, wb_ref, res_ref, g_ref, bb_ref, out_ref):
    m = (jnp.dot(a_ref[...].astype(BF16), wa_ref[...], preferred_element_type=F32)
         + jnp.dot(b_ref[...].astype(BF16), wb_ref[...], preferred_element_type=F32))
    out_ref[...] = _layer_norm_rows(DN_ALPHA * res_ref[...] + m, g_ref[...], bb_ref[...])


def _mix_out_ln(y_hy, y_mla, w_out, res, ln_g, ln_b, *, tm=512):
    T, D = res.shape
    tm = min(tm, T)
    wa = w_out[:HY_W].astype(BF16)
    wb = w_out[HY_W:].astype(BF16)
    row = lambda w: pl.BlockSpec((tm, w), lambda i: (i, 0))
    full = lambda a: pl.BlockSpec(a.shape, lambda i: (0, 0))
    vec = pl.BlockSpec((1, D), lambda i: (0, 0))
    return pl.pallas_call(
        _mix_out_ln_kernel,
        grid=(T // tm,),
        in_specs=[row(HY_W), row(MLA_V_W), full(wa), full(wb), row(D), vec, vec],
        out_specs=row(D),
        out_shape=jax.ShapeDtypeStruct((T, D), F32),
        compiler_params=_cparams(("parallel",)),
        name="mix_out_ln",
    )(y_hy, y_mla, wa, wb, res, ln_g.reshape(1, D), ln_b.reshape(1, D))


def _hyena_mla_layer(x, p, kspec, tabs, ln_g, ln_b):
    B, L, D = x.shape
    u, q, k, v = _in_proj(x, p['w_in'], p['w_uq'], p['w_ukv'], p['q_norm'], p['kv_norm'])
    y_hy = _hyena(u, p['conv_w'], p['conv_b'], p['skip'], kspec, tabs)
    y_mla = _attention(q, k, v)
    return _mix_out_ln(y_hy.reshape(B * L, HY_W), y_mla.reshape(B * L, MLA_V_W), p['w_out'],
                       x.reshape(B * L, D), ln_g, ln_b).reshape(B, L, D)


MOE_ROWS = 256
ROUTE_W = LANES
NEG_BIG = -1e30


def _router_kernel(x_ref, w_ref, idx_out, gate_out, rank_out, cnt_out, cnt_ref):
    logits = _hdot(x_ref[...], w_ref[...])
    lane = lax.broadcasted_iota(jnp.int32, logits.shape, 1)
    is_group = lane < MOE_GROUPS
    gl = jnp.where(is_group, logits, NEG_BIG)
    gmax = jnp.max(gl, axis=-1, keepdims=True)
    g_top = jnp.min(jnp.where(gl == gmax, lane, ROUTE_W), axis=-1, keepdims=True)
    g_prob = 1.0 / jnp.sum(jnp.where(is_group, jnp.exp(gl - gmax), 0.0), axis=-1, keepdims=True)

    first = MOE_GROUPS + g_top * MOE_PER_GROUP
    in_grp = jnp.logical_and(lane >= first, lane < first + MOE_PER_GROUP)
    el = jnp.where(in_grp, logits, NEG_BIG)
    emax = jnp.max(el, axis=-1, keepdims=True)
    ex = jnp.where(in_grp, jnp.exp(el - emax), 0.0)
    prob = ex / jnp.sum(ex, axis=-1, keepdims=True)
    pm = jnp.where(in_grp, prob, -1.0)
    p1 = jnp.max(pm, axis=-1, keepdims=True)
    l1 = jnp.min(jnp.where(pm == p1, lane, ROUTE_W), axis=-1, keepdims=True)
    pm2 = jnp.where(lane == l1, -1.0, pm)
    p2 = jnp.max(pm2, axis=-1, keepdims=True)
    l2 = jnp.min(jnp.where(pm2 == p2, lane, ROUTE_W), axis=-1, keepdims=True)
    tot = p1 + p2
    e1 = l1 - MOE_GROUPS
    e2 = l2 - MOE_GROUPS
    idx_out[...] = jnp.where(lane == 0, e1, jnp.where(lane == 1, e2, 0))
    gate_out[...] = jnp.where(lane == 0, g_prob * (p1 / tot), jnp.where(lane == 1, g_prob * (p2 / tot), 0.0))

    @pl.when(pl.program_id(0) == 0)
    def _():
        cnt_ref[...] = jnp.zeros_like(cnt_ref)

    tm = logits.shape[0]
    oh1 = jnp.where(lane == l1, 1.0, 0.0)
    oh2 = jnp.where(lane == l2, 1.0, 0.0)
    both = oh1 + oh2
    ri = lax.broadcasted_iota(jnp.int32, (tm, tm), 0)
    ci = lax.broadcasted_iota(jnp.int32, (tm, tm), 1)
    earlier = jnp.where(ci < ri, 1.0, 0.0).astype(BF16)
    before = cnt_ref[...] + jnp.dot(earlier, both.astype(BF16), preferred_element_type=F32)
    r1 = jnp.sum(oh1 * before, axis=-1, keepdims=True)
    r2 = jnp.sum(oh2 * before, axis=-1, keepdims=True)
    rank_out[...] = jnp.where(lane == 0, r1, jnp.where(lane == 1, r2, 0.0)).astype(jnp.int32)
    cnt_ref[...] += jnp.sum(both, axis=0, keepdims=True)
    cnt_out[...] = cnt_ref[...]


def _router(xt, w_group, w_expert, *, tm=512):
    T, D = xt.shape
    tm = min(tm, T)
    w = _pad_axis(jnp.concatenate([w_group, w_expert], axis=1), 1, ROUTE_W)
    row = pl.BlockSpec((tm, ROUTE_W), lambda i: (i, 0))
    one = pl.BlockSpec((1, ROUTE_W), lambda i: (0, 0))
    return pl.pallas_call(
        _router_kernel,
        grid=(T // tm,),
        in_specs=[pl.BlockSpec((tm, D), lambda i: (i, 0)), pl.BlockSpec((D, ROUTE_W), lambda i: (0, 0))],
        out_specs=[row, row, row, one],
        out_shape=[jax.ShapeDtypeStruct((T, ROUTE_W), jnp.int32), jax.ShapeDtypeStruct((T, ROUTE_W), F32),
                   jax.ShapeDtypeStruct((T, ROUTE_W), jnp.int32), jax.ShapeDtypeStruct((1, ROUTE_W), F32)],
        scratch_shapes=[pltpu.VMEM((1, ROUTE_W), F32)],
        compiler_params=_cparams(("arbitrary",)),
        name="moe_router",
    )(xt, w)


MOE_TILE = 256


def _tile_major(dest, tm):
    nt = dest.shape[0] // tm
    return dest.reshape(nt, tm, MOE_TOPK).transpose(0, 2, 1).reshape(nt, 1, MOE_TOPK * tm)


def _dispatch_kernel(dest_ref, x_ref, rows_in, rows_out, sem):
    del rows_in
    tm = x_ref.shape[0]

    def row_copy(k, r):
        return pltpu.make_async_copy(x_ref.at[pl.ds(r, 1)], rows_out.at[pl.ds(dest_ref[0, k * tm + r], 1)], sem)

    def start(r, c):
        for k in range(MOE_TOPK):
            row_copy(k, r).start()
        return c

    def wait(r, c):
        for k in range(MOE_TOPK):
            row_copy(k, r).wait()
        return c

    lax.fori_loop(0, tm, start, 0, unroll=8)
    lax.fori_loop(0, tm, wait, 0, unroll=8)


def _dispatch(xt, dest, n_rows):
    T, D = xt.shape
    tm = min(MOE_TILE, T)
    return pl.pallas_call(
        _dispatch_kernel,
        grid=(T // tm,),
        in_specs=[pl.BlockSpec((None, 1, MOE_TOPK * tm), lambda i: (i, 0, 0), memory_space=pltpu.SMEM),
                  pl.BlockSpec((tm, D), lambda i: (i, 0)), pl.BlockSpec(memory_space=pl.ANY)],
        out_specs=pl.BlockSpec(memory_space=pl.ANY),
        out_shape=jax.ShapeDtypeStruct((n_rows, D), F32),
        scratch_shapes=[pltpu.SemaphoreType.DMA(())],
        input_output_aliases={2: 0},
        compiler_params=_cparams(("arbitrary",)),
        name="moe_dispatch",
    )(_tile_major(dest, tm), xt, jnp.zeros((n_rows, D), F32))


def _expert_kernel(blk_e_ref, n_used_ref, x_ref, w1_ref, w3_ref, w2_ref, o_ref):
    i = pl.program_id(0)

    @pl.when(i < n_used_ref[0])
    def _():
        xb = x_ref[...].astype(BF16)
        h1 = jnp.dot(xb, w1_ref[...], preferred_element_type=F32)
        h3 = jnp.dot(xb, w3_ref[...], preferred_element_type=F32)
        h = (h1 * jax.nn.sigmoid(h1) * h3).astype(BF16)
        o_ref[...] = jnp.dot(h, w2_ref[...], preferred_element_type=F32)

    @pl.when(i >= n_used_ref[0])
    def _():
        o_ref[...] = jnp.zeros_like(o_ref)


def _experts(rows, blk_e, n_used, w1, w3, w2):
    P, D = rows.shape
    n_blocks = blk_e.shape[0]
    blk = lambda i, be, nu: (jnp.minimum(i, nu[0] - 1), 0)
    wspec = lambda shape: pl.BlockSpec((None,) + shape, lambda i, be, nu: (be[jnp.minimum(i, nu[0] - 1)], 0, 0))
    return pl.pallas_call(
        _expert_kernel,
        grid_spec=pltpu.PrefetchScalarGridSpec(
            num_scalar_prefetch=2,
            grid=(n_blocks,),
            in_specs=[pl.BlockSpec((MOE_ROWS, D), blk), wspec((D, MOE_FF)), wspec((D, MOE_FF)), wspec((MOE_FF, D))],
            out_specs=pl.BlockSpec((MOE_ROWS, D), lambda i, be, nu: (i, 0)),
        ),
        out_shape=jax.ShapeDtypeStruct((P, D), F32),
        compiler_params=_cparams(("arbitrary",)),
        name="moe_experts",
    )(blk_e, n_used, rows, w1, w3, w2)


def _combine_ln_kernel(dest_ref, dest_next_ref, eo_hbm, gate_ref, x_ref, g_ref, b_ref, out_ref, buf, sems):
    i = pl.program_id(0)
    tm = x_ref.shape[0]
    slot = lax.rem(i, 2)
    base = lambda s, k: pl.multiple_of((s * MOE_TOPK + k) * tm, tm)

    def row_copy(idx_ref, k, r, s):
        return pltpu.make_async_copy(eo_hbm.at[pl.ds(idx_ref[0, k * tm + r], 1)],
                                     buf.at[pl.ds(base(s, k) + r, 1)], sems.at[s])

    def start_all(idx_ref, s):
        def body(r, c):
            for k in range(MOE_TOPK):
                row_copy(idx_ref, k, r, s).start()
            return c
        lax.fori_loop(0, tm, body, 0, unroll=8)

    @pl.when(i == 0)
    def _():
        start_all(dest_ref, slot)

    @pl.when(i + 1 < pl.num_programs(0))
    def _():
        start_all(dest_next_ref, 1 - slot)

    def wait(r, c):
        for k in range(MOE_TOPK):
            row_copy(dest_ref, k, r, slot).wait()
        return c

    lax.fori_loop(0, tm, wait, 0, unroll=8)
    gate = gate_ref[...]
    f = gate[:, 0:1] * buf[pl.ds(base(slot, 0), tm), :] + gate[:, 1:2] * buf[pl.ds(base(slot, 1), tm), :]
    out_ref[...] = _layer_norm_rows(DN_ALPHA * x_ref[...] + f, g_ref[...], b_ref[...])


def _combine_ln(eo, dest, gate, xt, ln_g, ln_b):
    T, D = xt.shape
    tm = min(MOE_TILE, T)
    nt = T // tm
    dest_t = _tile_major(dest, tm)
    row = pl.BlockSpec((tm, D), lambda i: (i, 0))
    vec = pl.BlockSpec((1, D), lambda i: (0, 0))
    idx = lambda off: pl.BlockSpec((None, 1, MOE_TOPK * tm), lambda i: (jnp.minimum(i + off, nt - 1), 0, 0),
                                   memory_space=pltpu.SMEM)
    return pl.pallas_call(
        _combine_ln_kernel,
        grid=(nt,),
        in_specs=[idx(0), idx(1), pl.BlockSpec(memory_space=pl.ANY), pl.BlockSpec((tm, ROUTE_W), lambda i: (i, 0)),
                  row, vec, vec],
        out_specs=row,
        out_shape=jax.ShapeDtypeStruct((T, D), F32),
        scratch_shapes=[pltpu.VMEM((2 * MOE_TOPK * tm, D), F32), pltpu.SemaphoreType.DMA((2,))],
        compiler_params=_cparams(("arbitrary",)),
        name="moe_combine_ln",
    )(dest_t, dest_t, eo, gate, xt, ln_g.reshape(1, D), ln_b.reshape(1, D))


def _dispatch_plan(expert, rank, counts, n_blocks):
    cnt = counts[0, MOE_GROUPS:MOE_GROUPS + MOE_EXPERTS].astype(jnp.int32)
    padded = (cnt + MOE_ROWS - 1) // MOE_ROWS * MOE_ROWS
    pad_end = jnp.cumsum(padded)
    pad_start = pad_end - padded
    dest = jnp.take(pad_start, expert, axis=0) + rank
    first_row = jnp.arange(n_blocks, dtype=jnp.int32) * MOE_ROWS
    blk_e = jnp.minimum(jnp.sum(first_row[:, None] >= pad_end[None, :], axis=1), MOE_EXPERTS - 1).astype(jnp.int32)
    n_used = (pad_end[-1] // MOE_ROWS).astype(jnp.int32).reshape(1)
    return dest.astype(jnp.int32), blk_e, n_used


def _moe_layer(x, w_group, w_expert, w1, w3, w2, ln_g, ln_b):
    B, L, D = x.shape
    xt = x.reshape(B * L, D)
    T = B * L
    n_blocks = T * MOE_TOPK // MOE_ROWS + MOE_EXPERTS
    idx, gate, rank, counts = _router(xt, w_group, w_expert)
    dest, blk_e, n_used = _dispatch_plan(idx[:, :MOE_TOPK], rank[:, :MOE_TOPK], counts, n_blocks)
    rows = _dispatch(xt, dest, n_blocks * MOE_ROWS)
    eo = _experts(rows, blk_e, n_used, w1, w3, w2)
    return _combine_ln(eo, dest, gate, xt, ln_g, ln_b).reshape(B, L, D)


def _trunk(x, p, tabs, kspecs):
    for layer in range(DEPTH):
        i = layer // 2
        if layer % 2 == 0:
            lp = {n: p[n][i] for n in ('w_in', 'w_uq', 'w_ukv', 'q_norm', 'kv_norm', 'conv_w', 'conv_b', 'skip', 'w_out')}
            x = _hyena_mla_layer(x, lp, kspecs[i], tabs, p['ln1_g'][layer], p['ln1_b'][layer])
        else:
            lp = {n[3:]: p[n][i] for n in p if n.startswith('rw_')}
            x = _rwkv_layer(x, lp, p['ln1_g'][layer], p['ln1_b'][layer])
        x = _moe_layer(x, p['moe_w_group'][layer], p['moe_w_expert'][layer], p['moe_w1'][layer],
                       p['moe_w3'][layer], p['moe_w2'][layer], p['ln2_g'][layer], p['ln2_b'][layer])
    return x


def kernel(x_prompt, x_sample, ln1_g, ln1_b, ln2_g, ln2_b, mix_w_in, hy_conv_w, hy_conv_b, hy_ffn_w1,
           hy_ffn_b1, hy_ffn_w2, hy_ffn_b2, hy_ffn_w3, hy_sin_freq, hy_decay, hy_skip, mla_q_norm, mla_w_uq,
           mla_kv_norm, mla_w_ukv, mix_w_out, rw_mu, rw_w_r, rw_w_k, rw_w_v, rw_w0, rw_w1, rw_w2, rw_a0,
           rw_a1, rw_a2, rw_g1, rw_g2, rw_k_k, rw_k_a, rw_r_k, rw_gn_g, rw_gn_b, rw_w_o, moe_w_group,
           moe_w_expert, moe_w1, moe_w3, moe_w2):
    p = dict(ln1_g=ln1_g, ln1_b=ln1_b, ln2_g=ln2_g, ln2_b=ln2_b, w_in=mix_w_in, conv_w=hy_conv_w,
             conv_b=hy_conv_b, skip=hy_skip, q_norm=mla_q_norm, w_uq=mla_w_uq, kv_norm=mla_kv_norm,
             w_ukv=mla_w_ukv, w_out=mix_w_out, rw_mu=rw_mu, rw_w_r=rw_w_r, rw_w_k=rw_w_k, rw_w_v=rw_w_v,
             rw_w0=rw_w0, rw_w1=rw_w1, rw_w2=rw_w2, rw_a0=rw_a0, rw_a1=rw_a1, rw_a2=rw_a2, rw_g1=rw_g1,
             rw_g2=rw_g2, rw_k_k=rw_k_k, rw_k_a=rw_k_a, rw_r_k=rw_r_k, rw_gn_g=rw_gn_g, rw_gn_b=rw_gn_b,
             rw_w_o=rw_w_o, moe_w_group=moe_w_group, moe_w_expert=moe_w_expert,
             moe_w1=moe_w1.astype(BF16), moe_w3=moe_w3.astype(BF16), moe_w2=moe_w2.astype(BF16))
    outs = []
    for x in (x_prompt, x_sample):
        L = x.shape[1]
        tabs = _dft_tables(L)
        kspecs = [_hyena_filter_spectrum(L, hy_ffn_w1[i], hy_ffn_b1[i], hy_ffn_w2[i], hy_ffn_b2[i],
                                         hy_ffn_w3[i], hy_sin_freq[i], hy_decay[i], tabs)
                  for i in range((DEPTH + 1) // 2)]
        outs.append(_trunk(x, p, tabs, kspecs))
    return tuple(outs)
```
